```python
import jax, jax.numpy as jnp
from jax import lax
import numpy as np

D_MODEL = 1024
BATCH = 2
SEQ = 8192
DEPTH = 2
DEC_BATCH = 8
DEC_SEQ = 64
PAST_LEN = 2048

CHUNK = 64
N_SB_HEADS = 8
SB_HEAD_DIM = D_MODEL // N_SB_HEADS
D_SB = N_SB_HEADS * SB_HEAD_DIM
SB_BLOCK = 128
N_SG_GROUPS = 4
D_SG = D_MODEL
SG_GROUP_DIM = D_SG // N_SG_GROUPS
SG_CHUNK = 128
N_MEM = 256
N_MEM_HEADS = 4
MEM_HEAD_DIM = D_MODEL // N_MEM_HEADS
D_MEMQ = N_MEM_HEADS * MEM_HEAD_DIM
N_BRANCH = 3
D_FF = -(-8 * D_MODEL // (3 * 256)) * 256
IN_SIZES = (D_SB, D_SB, D_SB, D_SG, D_SG, D_MEMQ)
D_IN = sum(IN_SIZES) + N_BRANCH * D_MODEL
EPS = 1e-6

kernel_name = "stickbreak_gmlp_memory_stream_encoder"


def rmsnorm(x, g):
    xf = x.astype(jnp.float32)
    y = xf * lax.rsqrt(jnp.mean(xf * xf, axis=-1, keepdims=True) + EPS)
    return (y * g.astype(jnp.float32)).astype(x.dtype)


def layernorm(x, g, b):
    xf = x.astype(jnp.float32)
    mu = jnp.mean(xf, axis=-1, keepdims=True)
    var = jnp.mean(jnp.square(xf - mu), axis=-1, keepdims=True)
    y = (xf - mu) * lax.rsqrt(var + EPS)
    return (y * g.astype(jnp.float32) + b.astype(jnp.float32)).astype(x.dtype)


def stick_breaking(q, k, v, q_pos, k_pos):
    z = jnp.einsum('bqhd,bkhd->bhqk', q.astype(jnp.float32), k.astype(jnp.float32)) * (SB_HEAD_DIM ** -0.5)
    mask = k_pos[None, :] < q_pos[:, None]
    log_keep = jnp.where(mask, jax.nn.log_sigmoid(-z), 0.0)
    between = lax.cumsum(log_keep, axis=3, reverse=True) - log_keep
    w = jnp.where(mask, jnp.exp(jax.nn.log_sigmoid(z) + between), 0.0)
    return jnp.einsum('bhqk,bkhd->bqhd', w, v.astype(jnp.float32)).astype(v.dtype)


def sb_prompt(q, k, v):
    b, s, h, d = q.shape
    nb = s // SB_BLOCK
    qb = q.reshape(b, nb, SB_BLOCK, h, d).transpose(1, 0, 2, 3, 4)
    k_pos = jnp.arange(s)

    def block(args):
        q_blk, i = args
        q_pos = i * SB_BLOCK + jnp.arange(SB_BLOCK)
        return stick_breaking(q_blk, k, v, q_pos, k_pos)

    out = lax.map(block, (qb, jnp.arange(nb)))
    return out.transpose(1, 0, 2, 3, 4).reshape(b, s, h, d)


def sb_sample(q, k, v, cache_k, cache_v):
    t = q.shape[1]
    k_all = jnp.concatenate([cache_k.astype(k.dtype), k], axis=1)
    v_all = jnp.concatenate([cache_v.astype(v.dtype), v], axis=1)
    q_pos = PAST_LEN + jnp.arange(t)
    k_pos = jnp.arange(PAST_LEN + t)
    return stick_breaking(q, k_all, v_all, q_pos, k_pos)


def spatial_prompt(v, w_s, b_s):
    b, s, _ = v.shape
    vb = v.reshape(b, s // SG_CHUNK, SG_CHUNK, N_SG_GROUPS, SG_GROUP_DIM)
    w = jnp.tril(w_s).astype(v.dtype)
    out = jnp.einsum('gts,bnsgc->bntgc', w, vb) + b_s.T.astype(v.dtype)[None, None, :, :, None]
    return out.reshape(b, s, D_SG)


def spatial_sample(v, w_s, b_s):
    b, t, _ = v.shape
    vg = v.reshape(b, t, N_SG_GROUPS, SG_GROUP_DIM)
    w = jnp.tril(w_s)[:, :t, :t].astype(v.dtype)
    out = jnp.einsum('gts,bsgc->btgc', w, vg) + b_s[:, :t].T.astype(v.dtype)[None, :, :, None]
    return out.reshape(b, t, D_SG)


def memory_kv(mem, g_mem, w_mem_kv):
    b = mem.shape[0]
    kv = rmsnorm(mem, g_mem) @ w_mem_kv
    k, v = jnp.split(kv, 2, axis=-1)
    return (k.reshape(b, N_MEM, N_MEM_HEADS, MEM_HEAD_DIM), v.reshape(b, N_MEM, N_MEM_HEADS, MEM_HEAD_DIM))


def cross_attend(q, k, v):
    s = jnp.einsum('bshd,bmhd->bhsm', q.astype(jnp.float32), k.astype(jnp.float32)) * (MEM_HEAD_DIM ** -0.5)
    p = jax.nn.softmax(s, axis=-1)
    return jnp.einsum('bhsm,bmhd->bshd', p, v.astype(jnp.float32)).astype(q.dtype)


def mixer_sublayer(x, sb_fn, sg_fn, mem_k, mem_v, g_pre, w_in, b_gate, ln_g, ln_b, w_branch, w_out, g_post):
    lead = x.shape[:2]
    h = rmsnorm(x, g_pre)
    cuts = [int(c) for c in np.cumsum(IN_SIZES)]
    q_sb, k_sb, v_sb, u_sg, v_sg, q_mem, z_gate = jnp.split(h @ w_in, cuts, axis=-1)
    q_sb = q_sb.reshape(*lead, N_SB_HEADS, SB_HEAD_DIM)
    k_sb = k_sb.reshape(*lead, N_SB_HEADS, SB_HEAD_DIM)
    v_sb = v_sb.reshape(*lead, N_SB_HEADS, SB_HEAD_DIM)
    o_sb = sb_fn(q_sb, k_sb, v_sb).reshape(*lead, D_SB)
    v_n = layernorm(jax.nn.gelu(v_sg), ln_g, ln_b)
    o_sg = jax.nn.gelu(u_sg) * sg_fn(v_n)
    o_mem = cross_attend(q_mem.reshape(*lead, N_MEM_HEADS, MEM_HEAD_DIM), mem_k, mem_v).reshape(*lead, D_MEMQ)
    gates = jax.nn.sigmoid((z_gate + b_gate).astype(jnp.float32)).astype(x.dtype)
    gates = gates.reshape(*lead, N_BRANCH, D_MODEL)
    merged = sum(gates[..., i, :] * (o @ w_branch[i]) for i, o in enumerate((o_sb, o_sg, o_mem)))
    return x + rmsnorm(merged @ w_out, g_post), k_sb, v_sb, v_n


def ffn_sublayer(x, g_pre, w_ffn_in, w_ffn_out, g_post):
    a, b = jnp.split(rmsnorm(x, g_pre) @ w_ffn_in, 2, axis=-1)
    return x + rmsnorm((jax.nn.silu(a) * b) @ w_ffn_out, g_post)


def setup_inputs(seed: int = 0) -> dict:
    key = jax.random.key(seed)
    ks = iter(jax.random.split(key, 32))
    nrm = lambda shape, scale: jax.random.normal(next(ks), shape, jnp.float32) * scale
    gain = lambda shape: 1.0 + nrm(shape, 0.01)
    return {
        "x_prompt": nrm((BATCH, SEQ, D_MODEL), 1.0),
        "x_sample": nrm((DEC_BATCH, DEC_SEQ, D_MODEL), 1.0),
        "cache_sb_k": nrm((DEPTH, DEC_BATCH, PAST_LEN, N_SB_HEADS, SB_HEAD_DIM), 1.0),
        "cache_sb_v": nrm((DEPTH, DEC_BATCH, PAST_LEN, N_SB_HEADS, SB_HEAD_DIM), 1.0),
        "cache_mem_k": nrm((DEPTH, DEC_BATCH, N_MEM, N_MEM_HEADS, MEM_HEAD_DIM), 1.0),
        "cache_mem_v": nrm((DEPTH, DEC_BATCH, N_MEM, N_MEM_HEADS, MEM_HEAD_DIM), 1.0),
        "mem_prompt": nrm((BATCH, N_MEM, D_MODEL), 1.0),
        "g_pre_mix": gain((DEPTH, D_MODEL)),
        "w_in": nrm((DEPTH, D_MODEL, D_IN), D_MODEL ** -0.5),
        "b_gate": nrm((DEPTH, N_BRANCH * D_MODEL), 0.01),
        "ln_sg_g": gain((DEPTH, D_SG)),
        "ln_sg_b": nrm((DEPTH, D_SG), 0.01),
        "w_spatial": nrm((DEPTH, N_SG_GROUPS, SG_CHUNK, SG_CHUNK), SG_CHUNK ** -0.5),
        "b_spatial": gain((DEPTH, N_SG_GROUPS, SG_CHUNK)),
        "g_mem": gain((DEPTH, D_MODEL)),
        "w_mem_kv": nrm((DEPTH, D_MODEL, 2 * D_MEMQ), D_MODEL ** -0.5),
        "w_branch": nrm((DEPTH, N_BRANCH, D_MODEL, D_MODEL), D_MODEL ** -0.5),
        "w_out": nrm((DEPTH, D_MODEL, D_MODEL), D_MODEL ** -0.5),
        "g_post_mix": gain((DEPTH, D_MODEL)),
        "g_pre_ffn": gain((DEPTH, D_MODEL)),
        "w_ffn_in": nrm((DEPTH, D_MODEL, 2 * D_FF), D_MODEL ** -0.5),
        "w_ffn_out": nrm((DEPTH, D_FF, D_MODEL), D_FF ** -0.5),
        "g_post_ffn": gain((DEPTH, D_MODEL)),
    }


def reference(x_prompt, x_sample, cache_sb_k, cache_sb_v, cache_mem_k, cache_mem_v, mem_prompt,
              g_pre_mix, w_in, b_gate, ln_sg_g, ln_sg_b, w_spatial, b_spatial, g_mem, w_mem_kv,
              w_branch, w_out, g_post_mix, g_pre_ffn, w_ffn_in, w_ffn_out, g_post_ffn):
    xp, xs = x_prompt, x_sample
    sbk_p, sbv_p, mk_p, mv_p, sbk_s, sbv_s, sgv_s = [], [], [], [], [], [], []
    for l in range(DEPTH):
        mem_k, mem_v = memory_kv(mem_prompt, g_mem[l], w_mem_kv[l])
        xp, k_new, v_new, _ = mixer_sublayer(
            xp, sb_prompt, lambda v: spatial_prompt(v, w_spatial[l], b_spatial[l]), mem_k, mem_v,
            g_pre_mix[l], w_in[l], b_gate[l], ln_sg_g[l], ln_sg_b[l], w_branch[l], w_out[l], g_post_mix[l])
        xp = ffn_sublayer(xp, g_pre_ffn[l], w_ffn_in[l], w_ffn_out[l], g_post_ffn[l])
        sbk_p.append(k_new); sbv_p.append(v_new); mk_p.append(mem_k); mv_p.append(mem_v)
        xs, k_new, v_new, sg_v = mixer_sublayer(
            xs, lambda q, k, v: sb_sample(q, k, v, cache_sb_k[l], cache_sb_v[l]),
            lambda v: spatial_sample(v, w_spatial[l], b_spatial[l]), cache_mem_k[l], cache_mem_v[l],
            g_pre_mix[l], w_in[l], b_gate[l], ln_sg_g[l], ln_sg_b[l], w_branch[l], w_out[l], g_post_mix[l])
        xs = ffn_sublayer(xs, g_pre_ffn[l], w_ffn_in[l], w_ffn_out[l], g_post_ffn[l])
        sbk_s.append(k_new); sbv_s.append(v_new); sgv_s.append(sg_v)
    return (xp, xs, jnp.stack(sbk_p), jnp.stack(sbv_p), jnp.stack(mk_p), jnp.stack(mv_p),
            jnp.stack(sbk_s), jnp.stack(sbv_s), jnp.stack(sgv_s))
```

```python
import functools

import jax
import jax.numpy as jnp
from jax import lax
from jax.experimental import pallas as pl
from jax.experimental.pallas import tpu as pltpu

F32 = jnp.float32
BF16 = jnp.bfloat16

D_MODEL = 1024
CHUNK_COLS = 1024
N_SB_HEADS = 8
SB_HEAD_DIM = 128
N_SG_GROUPS = 4
SG_GROUP_DIM = 256
SG_CHUNK = 128
N_MEM = 256
N_MEM_HEADS = 4
MEM_HEAD_DIM = 256
N_BRANCH = 3
D_FF = 2816
EPS = 1e-6

COL_Q, COL_K, COL_V, COL_U, COL_VSG, COL_QMEM, COL_GATE = 0, 1, 2, 3, 4, 5, 6
N_COL_BLOCKS = COL_GATE + N_BRANCH

SB_KEY_BLOCK = 128
SB_LOG_WEIGHT_FLOOR = -104.0
FF_SPLIT = 2

VMEM_LIMIT = 56 * 1024 * 1024


def _dot(a, b):
    return jnp.dot(a, b, preferred_element_type=F32)


def _dot_nt(a, b):
    return lax.dot_general(a, b, (((1,), (1,)), ((), ())), preferred_element_type=F32)


def _rmsnorm(x, g):
    return x * lax.rsqrt(jnp.mean(x * x, axis=-1, keepdims=True) + EPS) * g


def _gelu(x):
    return 0.5 * x * (1.0 + jnp.tanh(0.7978845608028654 * (x + 0.044715 * (x * x * x))))


def _sigmoid(x):
    return 1.0 / (1.0 + jnp.exp(-x))


def _params(*semantics):
    return pltpu.CompilerParams(dimension_semantics=semantics, vmem_limit_bytes=VMEM_LIMIT)


def _in_proj_kernel(x_ref, g_ref, w_ref, bg_ref, lng_ref, lnb_ref, proj_ref, kv_ref, *rest,
                    emit_vn):
    if emit_vn:
        vn_ref, h_ref = rest
    else:
        (h_ref,) = rest
    j = pl.program_id(1)

    @pl.when(j == 0)
    def _():
        h_ref[...] = _rmsnorm(x_ref[...], g_ref[...]).astype(BF16)

    acc = _dot(h_ref[...], w_ref[...])

    @pl.when((j == COL_Q) | (j == COL_QMEM))
    def _():
        proj_ref[...] = acc.astype(BF16)

    @pl.when((j == COL_K) | (j == COL_V))
    def _():
        kv_ref[0] = acc
        proj_ref[...] = acc.astype(BF16)

    @pl.when(j == COL_U)
    def _():
        proj_ref[...] = _gelu(acc).astype(BF16)

    @pl.when(j == COL_VSG)
    def _():
        a = _gelu(acc)
        mu = jnp.mean(a, axis=-1, keepdims=True)
        d = a - mu
        var = jnp.mean(d * d, axis=-1, keepdims=True)
        vn = d * lax.rsqrt(var + EPS) * lng_ref[...] + lnb_ref[...]
        proj_ref[...] = vn.astype(BF16)
        if emit_vn:
            vn_ref[...] = vn

    @pl.when(j >= COL_GATE)
    def _():
        proj_ref[...] = _sigmoid(acc + bg_ref[...]).astype(BF16)


def _in_proj(x, g_pre, w_in, b_gate, ln_g, ln_b, *, tm, emit_vn):
    m = x.shape[0]
    row = lambda i, j: (i, 0)
    const = lambda i, j: (0, 0)
    out_shape = [jax.ShapeDtypeStruct((m, N_COL_BLOCKS * CHUNK_COLS), BF16),
                 jax.ShapeDtypeStruct((2, m, CHUNK_COLS), F32)]
    out_specs = [pl.BlockSpec((tm, CHUNK_COLS), lambda i, j: (i, j)),
                 pl.BlockSpec((1, tm, CHUNK_COLS), lambda i, j: (jnp.clip(j - COL_K, 0, 1), i, 0))]
    if emit_vn:
        out_shape.append(jax.ShapeDtypeStruct((m, CHUNK_COLS), F32))
        out_specs.append(pl.BlockSpec((tm, CHUNK_COLS), row))
    return pl.pallas_call(
        functools.partial(_in_proj_kernel, emit_vn=emit_vn),
        grid=(m // tm, N_COL_BLOCKS),
        in_specs=[pl.BlockSpec((tm, D_MODEL), row),
                  pl.BlockSpec((1, D_MODEL), const),
                  pl.BlockSpec((D_MODEL, CHUNK_COLS), lambda i, j: (0, j)),
                  pl.BlockSpec((1, CHUNK_COLS), lambda i, j: (0, jnp.maximum(j - COL_GATE, 0))),
                  pl.BlockSpec((1, CHUNK_COLS), const),
                  pl.BlockSpec((1, CHUNK_COLS), const)],
        out_specs=out_specs,
        out_shape=out_shape,
        scratch_shapes=[pltpu.VMEM((tm, D_MODEL), BF16)],
        compiler_params=_params("arbitrary", "arbitrary"),
        name="in_proj",
    )(x, g_pre, w_in, b_gate, ln_g, ln_b)


def _suffix_matrix():
    kk = SB_KEY_BLOCK
    j = lax.broadcasted_iota(jnp.int32, (kk, 2 * kk), 0)
    s = lax.broadcasted_iota(jnp.int32, (kk, 2 * kk), 1)
    return jnp.where((j > s) | (s >= kk), 1.0, 0.0).astype(BF16)


def _sb_step(q, k, v, mask, carry, u):
    z = _dot_nt(q, k) * (SB_HEAD_DIM ** -0.5)
    softplus = jnp.maximum(z, 0.0) + jnp.log1p(jnp.exp(-jnp.abs(z)))
    log_keep = -softplus
    if mask is not None:
        log_keep = jnp.where(mask, log_keep, 0.0)
    hi = log_keep.astype(BF16)
    rem = log_keep - hi.astype(F32)
    mid = rem.astype(BF16)
    lo = (rem - mid.astype(F32)).astype(BF16)
    sums = _dot(hi, u) + _dot(mid, u) + _dot(lo, u)
    between = carry + sums[:, :SB_KEY_BLOCK]
    w = jnp.exp((z - softplus) + between)
    if mask is not None:
        w = jnp.where(mask, w, 0.0)
    return _dot(w.astype(BF16), v), carry + sums[:, SB_KEY_BLOCK:]


def _sb_prompt_kernel(q_ref, k_ref, v_ref, o_ref, acc_ref, carry_ref, *, tq):
    kk = SB_KEY_BLOCK
    per_q = tq // kk
    u = _suffix_matrix()
    row = lax.broadcasted_iota(jnp.int32, (tq, kk), 0)
    col = lax.broadcasted_iota(jnp.int32, (tq, kk), 1)

    def q_block(qi, _):
        q = q_ref[pl.ds(pl.multiple_of(qi * tq, tq), tq), :]
        acc_ref[...] = jnp.zeros_like(acc_ref)
        carry_ref[...] = jnp.zeros_like(carry_ref)

        def step(kb, mask):
            ks = pl.ds(pl.multiple_of(kb * kk, kk), kk)
            pv, carry = _sb_step(q, k_ref[ks, :], v_ref[ks, :], mask, carry_ref[...], u)
            acc_ref[...] += pv
            carry_ref[...] = carry

        for d in range(per_q):
            off = (per_q - 1 - d) * kk
            step(qi * per_q + (per_q - 1 - d), col + off < row)

        def cond(state):
            kb, live = state
            return (kb >= 0) & live

        def body(state):
            kb, _ = state
            step(kb, None)
            return kb - 1, jnp.max(carry_ref[...]) >= SB_LOG_WEIGHT_FLOOR

        lax.while_loop(cond, body, (qi * per_q - 1, jnp.max(carry_ref[...]) >= SB_LOG_WEIGHT_FLOOR))
        o_ref[pl.ds(pl.multiple_of(qi * tq, tq), tq), :] = acc_ref[...].astype(BF16)
        return 0

    lax.fori_loop(0, q_ref.shape[0] // tq, q_block, 0)


def _sb_prompt(proj, *, batch, seq, tq):
    head = lambda c: pl.BlockSpec((seq, SB_HEAD_DIM), lambda b, h: (b, c * N_SB_HEADS + h))
    return pl.pallas_call(
        functools.partial(_sb_prompt_kernel, tq=tq),
        grid=(batch, N_SB_HEADS),
        in_specs=[head(COL_Q), head(COL_K), head(COL_V)],
        out_specs=pl.BlockSpec((seq, SB_HEAD_DIM), lambda b, h: (b, h)),
        out_shape=jax.ShapeDtypeStruct((batch * seq, N_SB_HEADS * SB_HEAD_DIM), BF16),
        scratch_shapes=[pltpu.VMEM((tq, SB_HEAD_DIM), F32), pltpu.VMEM((tq, SB_KEY_BLOCK), F32)],
        compiler_params=_params("arbitrary", "arbitrary"),
        name="sb_prompt",
    )(proj, proj, proj)


def _sb_sample_kernel(q_ref, k_ref, v_ref, ck_ref, cv_ref, o_ref, acc_ref, carry_ref):
    kk = SB_KEY_BLOCK
    t = q_ref.shape[0]
    past = ck_ref.shape[1]
    u = _suffix_matrix()
    q = q_ref[...]
    row = lax.broadcasted_iota(jnp.int32, (t, kk), 0)
    col = lax.broadcasted_iota(jnp.int32, (t, kk), 1)
    pad = jnp.zeros((kk - t, SB_HEAD_DIM), BF16)
    pv, carry = _sb_step(q, jnp.concatenate([k_ref[...], pad], axis=0),
                         jnp.concatenate([v_ref[...], pad], axis=0), col < row,
                         jnp.zeros((t, kk), F32), u)
    acc_ref[...] = pv
    carry_ref[...] = carry

    def cond(state):
        kb, live = state
        return (kb >= 0) & live

    def body(state):
        kb, _ = state
        ks = pl.ds(pl.multiple_of(kb * kk, kk), kk)
        pv, carry = _sb_step(q, ck_ref[0, ks, :].astype(BF16), cv_ref[0, ks, :].astype(BF16), None,
                             carry_ref[...], u)
        acc_ref[...] += pv
        carry_ref[...] = carry
        return kb - 1, jnp.max(carry) >= SB_LOG_WEIGHT_FLOOR

    lax.while_loop(cond, body, (past // kk - 1, jnp.max(carry) >= SB_LOG_WEIGHT_FLOOR))
    o_ref[...] = acc_ref[...].astype(BF16)


def _sb_sample(proj, cache_k, cache_v, *, batch, seq):
    past = cache_k.shape[1]
    head = lambda c: pl.BlockSpec((seq, SB_HEAD_DIM), lambda b, h: (b, c * N_SB_HEADS + h))
    cache = pl.BlockSpec((1, past, SB_HEAD_DIM), lambda b, h: (b, 0, h))
    return pl.pallas_call(
        _sb_sample_kernel,
        grid=(batch, N_SB_HEADS),
        in_specs=[head(COL_Q), head(COL_K), head(COL_V), cache, cache],
        out_specs=pl.BlockSpec((seq, SB_HEAD_DIM), lambda b, h: (b, h)),
        out_shape=jax.ShapeDtypeStruct((batch * seq, N_SB_HEADS * SB_HEAD_DIM), BF16),
        scratch_shapes=[pltpu.VMEM((seq, SB_HEAD_DIM), F32), pltpu.VMEM((seq, SB_KEY_BLOCK), F32)],
        compiler_params=_params("arbitrary", "arbitrary"),
        name="sb_sample",
    )(proj, proj, proj, cache_k, cache_v)


def _mixer_kernel(x_ref, osb_ref, u_ref, vn_ref, qm_ref, g0_ref, g1_ref, g2_ref, ws_ref, bs_ref,
                  mk_ref, mv_ref, wb_ref, wo_ref, gp_ref, y_ref, osg_ref, omem_ref, *, chunk, seg):
    tm = x_ref.shape[0]
    r = lax.broadcasted_iota(jnp.int32, (chunk, chunk), 0)
    c = lax.broadcasted_iota(jnp.int32, (chunk, chunk), 1)
    for g in range(N_SG_GROUPS):
        w_g = jnp.where(c <= r, ws_ref[g], 0.0).astype(BF16)
        b_g = bs_ref[:, g:g + 1]
        cols = slice(g * SG_GROUP_DIM, (g + 1) * SG_GROUP_DIM)
        for n in range(tm // chunk):
            rows = slice(n * chunk, (n + 1) * chunk)
            sg = _dot(w_g, vn_ref[rows, cols]) + b_g
            osg_ref[rows, cols] = (u_ref[rows, cols].astype(F32) * sg).astype(BF16)
    for s in range(tm // seg):
        rows = slice(s * seg, (s + 1) * seg)
        for h in range(N_MEM_HEADS):
            cols = slice(h * MEM_HEAD_DIM, (h + 1) * MEM_HEAD_DIM)
            sc = _dot_nt(qm_ref[rows, cols], mk_ref[s, :, cols]) * (MEM_HEAD_DIM ** -0.5)
            p = jnp.exp(sc - jnp.max(sc, axis=-1, keepdims=True))
            o = _dot(p.astype(BF16), mv_ref[s, :, cols]) / jnp.sum(p, axis=-1, keepdims=True)
            omem_ref[rows, cols] = o.astype(BF16)
    merged = (g0_ref[...].astype(F32) * _dot(osb_ref[...], wb_ref[0])
              + g1_ref[...].astype(F32) * _dot(osg_ref[...], wb_ref[1])
              + g2_ref[...].astype(F32) * _dot(omem_ref[...], wb_ref[2]))
    out = _dot(merged.astype(BF16), wo_ref[...])
    y_ref[...] = x_ref[...] + _rmsnorm(out, gp_ref[...])


def _mixer(x, o_sb, proj, w_s, b_s_t, mem_k, mem_v, w_branch, w_out, g_post, *, tm, chunk, seg,
           rows_per_mem):
    m = x.shape[0]
    row = lambda i: (i, 0)
    col = lambda cb: pl.BlockSpec((tm, CHUNK_COLS), lambda i: (i, cb))
    mem_blocks = tm // seg
    mem = pl.BlockSpec((mem_blocks, N_MEM, D_MODEL), lambda i: (i * tm // rows_per_mem // mem_blocks, 0, 0))
    const2 = lambda i: (0, 0)
    const3 = lambda i: (0, 0, 0)
    return pl.pallas_call(
        functools.partial(_mixer_kernel, chunk=chunk, seg=seg),
        grid=(m // tm,),
        in_specs=[pl.BlockSpec((tm, D_MODEL), row), pl.BlockSpec((tm, D_MODEL), row),
                  col(COL_U), col(COL_VSG), col(COL_QMEM), col(COL_GATE), col(COL_GATE + 1),
                  col(COL_GATE + 2),
                  pl.BlockSpec((N_SG_GROUPS, chunk, chunk), const3),
                  pl.BlockSpec((chunk, N_SG_GROUPS), const2),
                  mem, mem,
                  pl.BlockSpec((N_BRANCH, D_MODEL, D_MODEL), const3),
                  pl.BlockSpec((D_MODEL, D_MODEL), const2),
                  pl.BlockSpec((1, D_MODEL), const2)],
        out_specs=pl.BlockSpec((tm, D_MODEL), row),
        out_shape=jax.ShapeDtypeStruct((m, D_MODEL), F32),
        scratch_shapes=[pltpu.VMEM((tm, D_MODEL), BF16), pltpu.VMEM((tm, D_MODEL), BF16)],
        compiler_params=_params("arbitrary"),
        name="mixer",
    )(x, o_sb, proj, proj, proj, proj, proj, proj, w_s, b_s_t, mem_k, mem_v, w_branch, w_out, g_post)


def _ffn_kernel(x_ref, gpre_ref, wa_ref, wb_ref, wo_ref, gpost_ref, y_ref, h_ref, acc_ref):
    f = pl.program_id(1)

    @pl.when(f == 0)
    def _():
        h_ref[...] = _rmsnorm(x_ref[...], gpre_ref[...]).astype(BF16)

    h = h_ref[...]
    a = _dot(h, wa_ref[...])
    b = _dot(h, wb_ref[...])
    part = _dot((a * _sigmoid(a) * b).astype(BF16), wo_ref[...])

    @pl.when(f == 0)
    def _():
        acc_ref[...] = part

    @pl.when(f > 0)
    def _():
        acc_ref[...] += part

    @pl.when(f == FF_SPLIT - 1)
    def _():
        y_ref[...] = x_ref[...] + _rmsnorm(acc_ref[...], gpost_ref[...])


def _ffn(x, g_pre, w_ffn_in, w_ffn_out, g_post, *, tm):
    m = x.shape[0]
    fc = D_FF // FF_SPLIT
    row = lambda i, f: (i, 0)
    const = lambda i, f: (0, 0)
    return pl.pallas_call(
        _ffn_kernel,
        grid=(m // tm, FF_SPLIT),
        in_specs=[pl.BlockSpec((tm, D_MODEL), row),
                  pl.BlockSpec((1, D_MODEL), const),
                  pl.BlockSpec((D_MODEL, fc), lambda i, f: (0, f)),
                  pl.BlockSpec((D_MODEL, fc), lambda i, f: (0, FF_SPLIT + f)),
                  pl.BlockSpec((fc, D_MODEL), lambda i, f: (f, 0)),
                  pl.BlockSpec((1, D_MODEL), const)],
        out_specs=pl.BlockSpec((tm, D_MODEL), row),
        out_shape=jax.ShapeDtypeStruct((m, D_MODEL), F32),
        scratch_shapes=[pltpu.VMEM((tm, D_MODEL), BF16), pltpu.VMEM((tm, D_MODEL), F32)],
        compiler_params=_params("arbitrary", "arbitrary"),
        name="ffn",
    )(x, g_pre, w_ffn_in, w_ffn_in, w_ffn_out, g_post)


def _mem_kv_kernel(mem_ref, g_ref, w_ref, kv32_ref, kv16_ref):
    kv = _dot(_rmsnorm(mem_ref[...], g_ref[...]).astype(BF16), w_ref[...])
    kv32_ref[0] = kv
    kv16_ref[0] = kv.astype(BF16)


def _mem_kv(mem, g_mem, w_mem_kv):
    m = mem.shape[0]
    out = pl.BlockSpec((1, m, D_MODEL), lambda j: (j, 0, 0))
    return pl.pallas_call(
        _mem_kv_kernel,
        grid=(2,),
        in_specs=[pl.BlockSpec((m, D_MODEL), lambda j: (0, 0)),
                  pl.BlockSpec((1, D_MODEL), lambda j: (0, 0)),
                  pl.BlockSpec((D_MODEL, D_MODEL), lambda j: (0, j))],
        out_specs=[out, out],
        out_shape=[jax.ShapeDtypeStruct((2, m, D_MODEL), F32),
                   jax.ShapeDtypeStruct((2, m, D_MODEL), BF16)],
        compiler_params=_params("arbitrary"),
        name="mem_kv",
    )(mem, g_mem, w_mem_kv)


def kernel(x_prompt, x_sample, cache_sb_k, cache_sb_v, cache_mem_k, cache_mem_v, mem_prompt,
           g_pre_mix, w_in, b_gate, ln_sg_g, ln_sg_b, w_spatial, b_spatial, g_mem, w_mem_kv,
           w_branch, w_out, g_post_mix, g_pre_ffn, w_ffn_in, w_ffn_out, g_post_ffn):
    depth = w_in.shape[0]
    batch, seq, _ = x_prompt.shape
    dec_batch, dec_seq, _ = x_sample.shape
    past = cache_sb_k.shape[2]
    tm = 512
    assert (batch * seq) % tm == 0 and seq % tm == 0 and dec_batch * dec_seq == tm

    xp = x_prompt.reshape(batch * seq, D_MODEL)
    xs = x_sample.reshape(dec_batch * dec_seq, D_MODEL)
    mem = mem_prompt.reshape(batch * N_MEM, D_MODEL)
    vec = lambda a: a.reshape(1, -1)
    outs = [[] for _ in range(7)]
    for l in range(depth):
        w_in_l = w_in[l].astype(BF16)
        w_branch_l = w_branch[l].astype(BF16)
        w_out_l = w_out[l].astype(BF16)
        w_ffn_in_l = w_ffn_in[l].astype(BF16)
        w_ffn_out_l = w_ffn_out[l].astype(BF16)
        proj_args = (vec(g_pre_mix[l]), w_in_l, vec(b_gate[l]), vec(ln_sg_g[l]), vec(ln_sg_b[l]))
        ffn_args = (vec(g_pre_ffn[l]), w_ffn_in_l, w_ffn_out_l, vec(g_post_ffn[l]))

        kv32, kv16 = _mem_kv(mem, vec(g_mem[l]), w_mem_kv[l].astype(BF16))
        proj, sbkv = _in_proj(xp, *proj_args, tm=tm, emit_vn=False)
        o_sb = _sb_prompt(proj, batch=batch, seq=seq, tq=256)
        xp = _mixer(xp, o_sb, proj, w_spatial[l], b_spatial[l].T,
                    kv16[0].reshape(batch, N_MEM, D_MODEL), kv16[1].reshape(batch, N_MEM, D_MODEL),
                    w_branch_l, w_out_l, vec(g_post_mix[l]),
                    tm=tm, chunk=SG_CHUNK, seg=tm, rows_per_mem=seq)
        xp = _ffn(xp, *ffn_args, tm=tm)
        outs[0].append(sbkv[0].reshape(batch, seq, N_SB_HEADS, SB_HEAD_DIM))
        outs[1].append(sbkv[1].reshape(batch, seq, N_SB_HEADS, SB_HEAD_DIM))
        outs[2].append(kv32[0].reshape(batch, N_MEM, N_MEM_HEADS, MEM_HEAD_DIM))
        outs[3].append(kv32[1].reshape(batch, N_MEM, N_MEM_HEADS, MEM_HEAD_DIM))

        proj, sbkv, vn = _in_proj(xs, *proj_args, tm=tm, emit_vn=True)
        o_sb = _sb_sample(proj, cache_sb_k[l].reshape(dec_batch, past, N_SB_HEADS * SB_HEAD_DIM),
                          cache_sb_v[l].reshape(dec_batch, past, N_SB_HEADS * SB_HEAD_DIM),
                          batch=dec_batch, seq=dec_seq)
        xs = _mixer(xs, o_sb, proj, w_spatial[l][:, :dec_seq, :dec_seq], b_spatial[l][:, :dec_seq].T,
                    cache_mem_k[l].reshape(dec_batch, N_MEM, D_MODEL).astype(BF16),
                    cache_mem_v[l].reshape(dec_batch, N_MEM, D_MODEL).astype(BF16),
                    w_branch_l, w_out_l, vec(g_post_mix[l]),
                    tm=tm, chunk=dec_seq, seg=dec_seq, rows_per_mem=dec_seq)
        xs = _ffn(xs, *ffn_args, tm=tm)
        outs[4].append(sbkv[0].reshape(dec_batch, dec_seq, N_SB_HEADS, SB_HEAD_DIM))
        outs[5].append(sbkv[1].reshape(dec_batch, dec_seq, N_SB_HEADS, SB_HEAD_DIM))
        outs[6].append(vn.reshape(dec_batch, dec_seq, D_MODEL))

    return (xp.reshape(batch, seq, D_MODEL), xs.reshape(dec_batch, dec_seq, D_MODEL),
            *[jnp.stack(o) for o in outs])
```

```python
import functools

import jax
import jax.numpy as jnp
from jax import lax
from jax.experimental import pallas as pl
from jax.experimental.pallas import tpu as pltpu

F32 = jnp.float32
BF16 = jnp.bfloat16

D_MODEL = 1024
CHUNK_COLS = 1024
N_SB_HEADS = 8
SB_HEAD_DIM = 128
N_SG_GROUPS = 4
SG_GROUP_DIM = 256
SG_CHUNK = 128
N_MEM = 256
N_MEM_HEADS = 4
MEM_HEAD_DIM = 256
N_BRANCH = 3
D_FF = 2816
EPS = 1e-6

COL_Q, COL_K, COL_V, COL_U, COL_VSG, COL_QMEM, COL_GATE = 0, 1, 2, 3, 4, 5, 6
N_COL_BLOCKS = COL_GATE + N_BRANCH

SB_TILE = 128
SB_WINDOW = 3
SB_LOG_WEIGHT_FLOOR = -104.0
FF_SPLIT = 2

VMEM_LIMIT = 56 * 1024 * 1024


def _dot(a, b):
    return jnp.dot(a, b, preferred_element_type=F32)


def _dot_nt(a, b):
    return lax.dot_general(a, b, (((1,), (1,)), ((), ())), preferred_element_type=F32)


def _rmsnorm(x, g):
    return x * lax.rsqrt(jnp.mean(x * x, axis=-1, keepdims=True) + EPS) * g


def _gelu(x):
    return 0.5 * x * (1.0 + jnp.tanh(0.7978845608028654 * (x + 0.044715 * (x * x * x))))


def _sigmoid(x):
    return 1.0 / (1.0 + jnp.exp(-x))


def _params(*semantics):
    return pltpu.CompilerParams(dimension_semantics=semantics, vmem_limit_bytes=VMEM_LIMIT)


def _store_heads(dst_ref, acc):
    tm = acc.shape[0]
    for h in range(N_SB_HEADS):
        dst_ref[pl.ds(h, tm, stride=N_SB_HEADS), :] = acc[:, h * SB_HEAD_DIM:(h + 1) * SB_HEAD_DIM]


def _in_proj_kernel(*refs, n_prev):
    (x_ref, g_ref, w_ref, bg_ref, lng_ref, lnb_ref), refs = refs[:6], refs[6 + n_prev:]
    proj_ref, k_ref, v_ref, vn_ref, h_ref = refs
    j = pl.program_id(1)

    @pl.when(j == 0)
    def _():
        h_ref[...] = _rmsnorm(x_ref[...], g_ref[...]).astype(BF16)

    acc = _dot(h_ref[...], w_ref[...])

    @pl.when((j == COL_Q) | (j == COL_QMEM))
    def _():
        proj_ref[...] = acc.astype(BF16)

    @pl.when(j == COL_K)
    def _():
        _store_heads(k_ref, acc)
        proj_ref[...] = acc.astype(BF16)

    @pl.when(j == COL_V)
    def _():
        _store_heads(v_ref, acc)
        proj_ref[...] = acc.astype(BF16)

    @pl.when(j == COL_U)
    def _():
        proj_ref[...] = _gelu(acc).astype(BF16)

    @pl.when(j == COL_VSG)
    def _():
        a = _gelu(acc)
        mu = jnp.mean(a, axis=-1, keepdims=True)
        d = a - mu
        var = jnp.mean(d * d, axis=-1, keepdims=True)
        vn = d * lax.rsqrt(var + EPS) * lng_ref[...] + lnb_ref[...]
        proj_ref[...] = vn.astype(BF16)
        if vn_ref is not None:
            vn_ref[...] = vn

    @pl.when(j >= COL_GATE)
    def _():
        proj_ref[...] = _sigmoid(acc + bg_ref[...]).astype(BF16)


def _in_proj(x, g_pre, w_in, b_gate, ln_g, ln_b, prev, *, layer, depth, tm, emit_vn):
    m = x.shape[0]
    n_i = m // tm
    row = lambda i, j: (i, 0)
    const = lambda i, j: (0, 0)
    layer_row = lambda i, j: (layer * n_i + i, 0)
    out_shape = [jax.ShapeDtypeStruct((m, N_COL_BLOCKS * CHUNK_COLS), BF16),
                 jax.ShapeDtypeStruct((depth * m * N_SB_HEADS, SB_HEAD_DIM), F32),
                 jax.ShapeDtypeStruct((depth * m * N_SB_HEADS, SB_HEAD_DIM), F32)]
    out_specs = [pl.BlockSpec((tm, CHUNK_COLS), lambda i, j: (i, j)),
                 pl.BlockSpec((tm * N_SB_HEADS, SB_HEAD_DIM), layer_row),
                 pl.BlockSpec((tm * N_SB_HEADS, SB_HEAD_DIM), layer_row)]
    if emit_vn:
        out_shape.append(jax.ShapeDtypeStruct((depth * m, CHUNK_COLS), F32))
        out_specs.append(pl.BlockSpec((tm, CHUNK_COLS), layer_row))
    if not prev:
        prev = [jnp.zeros(s.shape, s.dtype) for s in out_shape[1:]]
    n_prev = len(prev)
    assert n_prev == len(out_shape) - 1

    def body(*refs):
        if not emit_vn:
            refs = refs[:-1] + (None, refs[-1])
        _in_proj_kernel(*refs, n_prev=n_prev)

    return pl.pallas_call(
        body,
        grid=(n_i, N_COL_BLOCKS),
        in_specs=[pl.BlockSpec((tm, D_MODEL), row),
                  pl.BlockSpec((1, D_MODEL), const),
                  pl.BlockSpec((D_MODEL, CHUNK_COLS), lambda i, j: (0, j)),
                  pl.BlockSpec((1, CHUNK_COLS), lambda i, j: (0, jnp.maximum(j - COL_GATE, 0))),
                  pl.BlockSpec((1, CHUNK_COLS), const),
                  pl.BlockSpec((1, CHUNK_COLS), const)]
                 + [pl.BlockSpec(memory_space=pl.ANY)] * n_prev,
        out_specs=out_specs,
        out_shape=out_shape,
        input_output_aliases={6 + p: 1 + p for p in range(n_prev)},
        scratch_shapes=[pltpu.VMEM((tm, D_MODEL), BF16)],
        compiler_params=_params("arbitrary", "arbitrary"),
        name="in_proj",
    )(x, g_pre, w_in, b_gate, ln_g, ln_b, *prev)


def _suffix_matrix():
    t = SB_TILE
    j = lax.broadcasted_iota(jnp.int32, (2 * t, 2 * t), 0) % t
    s = lax.broadcasted_iota(jnp.int32, (2 * t, 2 * t), 1)
    return jnp.where((j > s) | (s >= t), 1.0, 0.0).astype(BF16)


def _sb_tiles(q, k, v, n_tiles, carry, mask, u):
    z = _dot_nt(q, k) * (SB_HEAD_DIM ** -0.5)
    softplus = jnp.maximum(z, 0.0) + jnp.log1p(jnp.exp(-jnp.abs(z)))
    log_beta = z - softplus
    weights = [None] * n_tiles
    for d in reversed(range(n_tiles)):
        cols = slice(d * SB_TILE, (d + 1) * SB_TILE)
        masked = mask is not None and d == n_tiles - 1
        log_keep = -softplus[:, cols]
        if masked:
            log_keep = jnp.where(mask, log_keep, 0.0)
        hi = log_keep.astype(BF16)
        lo = (log_keep - hi.astype(F32)).astype(BF16)
        sums = _dot(jnp.concatenate([hi, lo], axis=1), u)
        suffix, total = sums[:, :SB_TILE], sums[:, SB_TILE:]
        between = suffix if carry is None else carry + suffix
        w = jnp.exp(log_beta[:, cols] + between)
        if masked:
            w = jnp.where(mask, w, 0.0)
        weights[d] = w.astype(BF16)
        carry = total if carry is None else carry + total
    return _dot(jnp.concatenate(weights, axis=1), v), carry


def _live(carry):
    return jnp.max(carry) >= SB_LOG_WEIGHT_FLOOR


def _sb_prompt_kernel(q_ref, k_ref, v_ref, o_ref, acc_ref, carry_ref):
    t = SB_TILE
    u = _suffix_matrix()
    causal = (lax.broadcasted_iota(jnp.int32, (t, t), 1) < lax.broadcasted_iota(jnp.int32, (t, t), 0))

    def rows(tile, n=1):
        start = tile * t
        return pl.ds(start if isinstance(start, int) else pl.multiple_of(start, t), n * t)

    def window(slot, qt, n_tiles):
        ks = rows(qt - (n_tiles - 1), n_tiles)
        pv, carry = _sb_tiles(q_ref[rows(qt), :], k_ref[ks, :], v_ref[ks, :], n_tiles, None, causal, u)
        acc_ref[slot] = pv
        carry_ref[slot] = carry
        return _live(carry)

    def finish(slot, qt, n_tiles, live):
        def body(state):
            kb, _ = state
            pv, carry = _sb_tiles(q_ref[rows(qt), :], k_ref[rows(kb), :], v_ref[rows(kb), :], 1,
                                  carry_ref[slot], None, u)
            acc_ref[slot] += pv
            carry_ref[slot] = carry
            return kb - 1, _live(carry)

        lax.while_loop(lambda s: (s[0] >= 0) & s[1], body, (qt - n_tiles, live))
        o_ref[rows(qt), :] = acc_ref[slot].astype(BF16)

    def tile_pair(qt, n_a, n_b):
        live_a = window(0, qt, n_a)
        live_b = window(1, qt + 1, n_b)
        finish(0, qt, n_a, live_a)
        finish(1, qt + 1, n_b, live_b)

    n_q = q_ref.shape[0] // t
    first = -(-(SB_WINDOW - 1) // 2) * 2
    for qt in range(0, first, 2):
        tile_pair(qt, min(qt + 1, SB_WINDOW), min(qt + 2, SB_WINDOW))

    def step(i, _):
        tile_pair(first + 2 * i, SB_WINDOW, SB_WINDOW)
        return 0

    lax.fori_loop(0, (n_q - first) // 2, step, 0)


def _sb_prompt(proj, *, batch, seq):
    assert seq % (2 * SB_TILE) == 0
    head = lambda c: pl.BlockSpec((seq, SB_HEAD_DIM), lambda b, h: (b, c * N_SB_HEADS + h))
    return pl.pallas_call(
        _sb_prompt_kernel,
        grid=(batch, N_SB_HEADS),
        in_specs=[head(COL_Q), head(COL_K), head(COL_V)],
        out_specs=pl.BlockSpec((seq, SB_HEAD_DIM), lambda b, h: (b, h)),
        out_shape=jax.ShapeDtypeStruct((batch * seq, N_SB_HEADS * SB_HEAD_DIM), BF16),
        scratch_shapes=[pltpu.VMEM((2, SB_TILE, SB_HEAD_DIM), F32), pltpu.VMEM((2, SB_TILE, SB_TILE), F32)],
        compiler_params=_params("arbitrary", "arbitrary"),
        name="sb_prompt",
    )(proj, proj, proj)


def _sb_sample_kernel(q_ref, k_ref, v_ref, ck_hbm, cv_hbm, o_ref, kbuf, vbuf, sem, acc_ref, carry_ref,
                      *, cache_row0, past):
    t = SB_TILE
    n_new = q_ref.shape[0]
    n_blocks = past // t
    block_rows = t * N_SB_HEADS
    u = _suffix_matrix()
    base = cache_row0 + pl.program_id(0) * past * N_SB_HEADS

    def copies(kb, slot):
        src = pl.ds(base + kb * block_rows, block_rows)
        return (pltpu.make_async_copy(ck_hbm.at[src, :], kbuf.at[slot], sem.at[0, slot]),
                pltpu.make_async_copy(cv_hbm.at[src, :], vbuf.at[slot], sem.at[1, slot]))

    def start(kb, slot):
        for c in copies(kb, slot):
            c.start()

    def wait(kb, slot):
        for c in copies(kb, slot):
            c.wait()

    start(n_blocks - 1, 0)

    causal = (lax.broadcasted_iota(jnp.int32, (n_new, t), 1) < lax.broadcasted_iota(jnp.int32, (n_new, t), 0))
    pad = jnp.zeros((t - n_new, SB_HEAD_DIM), BF16)
    live = False
    for h in range(N_SB_HEADS):
        cols = slice(h * SB_HEAD_DIM, (h + 1) * SB_HEAD_DIM)
        pv, carry = _sb_tiles(q_ref[:, cols], jnp.concatenate([k_ref[:, cols], pad], axis=0),
                              jnp.concatenate([v_ref[:, cols], pad], axis=0), 1, None, causal, u)
        acc_ref[h] = pv
        carry_ref[h] = carry
        live = live | _live(carry)

    def body(state):
        kb, _ = state
        slot = (n_blocks - 1 - kb) % 2
        wait(kb, slot)

        @pl.when(kb > 0)
        def _():
            start(kb - 1, 1 - slot)

        live = False
        for h in range(N_SB_HEADS):
            head_rows = pl.ds(h, t, stride=N_SB_HEADS)
            pv, carry = _sb_tiles(q_ref[:, h * SB_HEAD_DIM:(h + 1) * SB_HEAD_DIM],
                                  kbuf[slot, head_rows, :].astype(BF16),
                                  vbuf[slot, head_rows, :].astype(BF16), 1, carry_ref[h], None, u)
            acc_ref[h] += pv
            carry_ref[h] = carry
            live = live | _live(carry)
        return kb - 1, live

    kb_end, _ = lax.while_loop(lambda s: (s[0] >= 0) & s[1], body, (n_blocks - 1, live))

    @pl.when(kb_end >= 0)
    def _():
        wait(kb_end, (n_blocks - 1 - kb_end) % 2)

    for h in range(N_SB_HEADS):
        o_ref[:, h * SB_HEAD_DIM:(h + 1) * SB_HEAD_DIM] = acc_ref[h].astype(BF16)


def _sb_sample(proj, cache_k, cache_v, *, layer, batch, seq, past):
    assert past % SB_TILE == 0 and seq <= SB_TILE
    blk = lambda c: pl.BlockSpec((seq, CHUNK_COLS), lambda b: (b, c))
    block_rows = SB_TILE * N_SB_HEADS
    return pl.pallas_call(
        functools.partial(_sb_sample_kernel, cache_row0=layer * batch * past * N_SB_HEADS, past=past),
        grid=(batch,),
        in_specs=[blk(COL_Q), blk(COL_K), blk(COL_V),
                  pl.BlockSpec(memory_space=pl.ANY), pl.BlockSpec(memory_space=pl.ANY)],
        out_specs=pl.BlockSpec((seq, N_SB_HEADS * SB_HEAD_DIM), lambda b: (b, 0)),
        out_shape=jax.ShapeDtypeStruct((batch * seq, N_SB_HEADS * SB_HEAD_DIM), BF16),
        scratch_shapes=[pltpu.VMEM((2, block_rows, SB_HEAD_DIM), F32),
                        pltpu.VMEM((2, block_rows, SB_HEAD_DIM), F32),
                        pltpu.SemaphoreType.DMA((2, 2)),
                        pltpu.VMEM((N_SB_HEADS, seq, SB_HEAD_DIM), F32),
                        pltpu.VMEM((N_SB_HEADS, seq, SB_TILE), F32)],
        compiler_params=_params("arbitrary"),
        name="sb_sample",
    )(proj, proj, proj, cache_k, cache_v)


def _mixer_kernel(x_ref, osb_ref, u_ref, vn_ref, qm_ref, g0_ref, g1_ref, g2_ref, ws_ref, bs_ref,
                  mk_ref, mv_ref, wb_ref, wo_ref, gp_ref, y_ref, osg_ref, omem_ref, *, chunk, seg):
    tm = x_ref.shape[0]
    r = lax.broadcasted_iota(jnp.int32, (chunk, chunk), 0)
    c = lax.broadcasted_iota(jnp.int32, (chunk, chunk), 1)
    for g in range(N_SG_GROUPS):
        w_g = jnp.where(c <= r, ws_ref[g], 0.0).astype(BF16)
        b_g = bs_ref[:, g:g + 1]
        cols = slice(g * SG_GROUP_DIM, (g + 1) * SG_GROUP_DIM)
        for n in range(tm // chunk):
            rows = slice(n * chunk, (n + 1) * chunk)
            sg = _dot(w_g, vn_ref[rows, cols]) + b_g
            osg_ref[rows, cols] = (u_ref[rows, cols].astype(F32) * sg).astype(BF16)
    for s in range(tm // seg):
        rows = slice(s * seg, (s + 1) * seg)
        for h in range(N_MEM_HEADS):
            cols = slice(h * MEM_HEAD_DIM, (h + 1) * MEM_HEAD_DIM)
            sc = _dot_nt(qm_ref[rows, cols], mk_ref[s, :, cols]) * (MEM_HEAD_DIM ** -0.5)
            p = jnp.exp(sc - jnp.max(sc, axis=-1, keepdims=True))
            o = _dot(p.astype(BF16), mv_ref[s, :, cols]) / jnp.sum(p, axis=-1, keepdims=True)
            omem_ref[rows, cols] = o.astype(BF16)
    merged = (g0_ref[...].astype(F32) * _dot(osb_ref[...], wb_ref[0])
              + g1_ref[...].astype(F32) * _dot(osg_ref[...], wb_ref[1])
              + g2_ref[...].astype(F32) * _dot(omem_ref[...], wb_ref[2]))
    out = _dot(merged.astype(BF16), wo_ref[...])
    y_ref[...] = x_ref[...] + _rmsnorm(out, gp_ref[...])


def _mixer(x, o_sb, proj, w_s, b_s_t, mem_k, mem_v, w_branch, w_out, g_post, *, tm, chunk, seg,
           rows_per_mem):
    m = x.shape[0]
    row = lambda i: (i, 0)
    col = lambda cb: pl.BlockSpec((tm, CHUNK_COLS), lambda i: (i, cb))
    mem_blocks = tm // seg
    mem = pl.BlockSpec((mem_blocks, N_MEM, D_MODEL), lambda i: (i * tm // rows_per_mem // mem_blocks, 0, 0))
    const2 = lambda i: (0, 0)
    const3 = lambda i: (0, 0, 0)
    return pl.pallas_call(
        functools.partial(_mixer_kernel, chunk=chunk, seg=seg),
        grid=(m // tm,),
        in_specs=[pl.BlockSpec((tm, D_MODEL), row), pl.BlockSpec((tm, D_MODEL), row),
                  col(COL_U), col(COL_VSG), col(COL_QMEM), col(COL_GATE), col(COL_GATE + 1),
                  col(COL_GATE + 2),
                  pl.BlockSpec((N_SG_GROUPS, chunk, chunk), const3),
                  pl.BlockSpec((chunk, N_SG_GROUPS), const2),
                  mem, mem,
                  pl.BlockSpec((N_BRANCH, D_MODEL, D_MODEL), const3),
                  pl.BlockSpec((D_MODEL, D_MODEL), const2),
                  pl.BlockSpec((1, D_MODEL), const2)],
        out_specs=pl.BlockSpec((tm, D_MODEL), row),
        out_shape=jax.ShapeDtypeStruct((m, D_MODEL), F32),
        scratch_shapes=[pltpu.VMEM((tm, D_MODEL), BF16), pltpu.VMEM((tm, D_MODEL), BF16)],
        compiler_params=_params("arbitrary"),
        name="mixer",
    )(x, o_sb, proj, proj, proj, proj, proj, proj, w_s, b_s_t, mem_k, mem_v, w_branch, w_out, g_post)


def _ffn_kernel(x_ref, gpre_ref, wa_ref, wb_ref, wo_ref, gpost_ref, y_ref, h_ref, acc_ref):
    f = pl.program_id(1)

    @pl.when(f == 0)
    def _():
        h_ref[...] = _rmsnorm(x_ref[...], gpre_ref[...]).astype(BF16)

    h = h_ref[...]
    a = _dot(h, wa_ref[...])
    b = _dot(h, wb_ref[...])
    part = _dot((a * _sigmoid(a) * b).astype(BF16), wo_ref[...])

    @pl.when(f == 0)
    def _():
        acc_ref[...] = part

    @pl.when(f > 0)
    def _():
        acc_ref[...] += part

    @pl.when(f == FF_SPLIT - 1)
    def _():
        y_ref[...] = x_ref[...] + _rmsnorm(acc_ref[...], gpost_ref[...])


def _ffn(x, g_pre, w_ffn_in, w_ffn_out, g_post, *, tm):
    m = x.shape[0]
    fc = D_FF // FF_SPLIT
    row = lambda i, f: (i, 0)
    const = lambda i, f: (0, 0)
    return pl.pallas_call(
        _ffn_kernel,
        grid=(m // tm, FF_SPLIT),
        in_specs=[pl.BlockSpec((tm, D_MODEL), row),
                  pl.BlockSpec((1, D_MODEL), const),
                  pl.BlockSpec((D_MODEL, fc), lambda i, f: (0, f)),
                  pl.BlockSpec((D_MODEL, fc), lambda i, f: (0, FF_SPLIT + f)),
                  pl.BlockSpec((fc, D_MODEL), lambda i, f: (f, 0)),
                  pl.BlockSpec((1, D_MODEL), const)],
        out_specs=pl.BlockSpec((tm, D_MODEL), row),
        out_shape=jax.ShapeDtypeStruct((m, D_MODEL), F32),
        scratch_shapes=[pltpu.VMEM((tm, D_MODEL), BF16), pltpu.VMEM((tm, D_MODEL), F32)],
        compiler_params=_params("arbitrary", "arbitrary"),
        name="ffn",
    )(x, g_pre, w_ffn_in, w_ffn_in, w_ffn_out, g_post)


def _mem_kv_kernel(mem_ref, g_ref, w_ref, kv32_ref, kv16_ref):
    kv = _dot(_rmsnorm(mem_ref[...], g_ref[...]).astype(BF16), w_ref[...])
    kv32_ref[0] = kv
    kv16_ref[0] = kv.astype(BF16)


def _mem_kv(mem, g_mem, w_mem_kv):
    m = mem.shape[0]
    out = pl.BlockSpec((1, m, D_MODEL), lambda j: (j, 0, 0))
    return pl.pallas_call(
        _mem_kv_kernel,
        grid=(2,),
        in_specs=[pl.BlockSpec((m, D_MODEL), lambda j: (0, 0)),
                  pl.BlockSpec((1, D_MODEL), lambda j: (0, 0)),
                  pl.BlockSpec((D_MODEL, D_MODEL), lambda j: (0, j))],
        out_specs=[out, out],
        out_shape=[jax.ShapeDtypeStruct((2, m, D_MODEL), F32),
                   jax.ShapeDtypeStruct((2, m, D_MODEL), BF16)],
        compiler_params=_params("arbitrary"),
        name="mem_kv",
    )(mem, g_mem, w_mem_kv)


def kernel(x_prompt, x_sample, cache_sb_k, cache_sb_v, cache_mem_k, cache_mem_v, mem_prompt,
           g_pre_mix, w_in, b_gate, ln_sg_g, ln_sg_b, w_spatial, b_spatial, g_mem, w_mem_kv,
           w_branch, w_out, g_post_mix, g_pre_ffn, w_ffn_in, w_ffn_out, g_post_ffn):
    depth = w_in.shape[0]
    batch, seq, _ = x_prompt.shape
    dec_batch, dec_seq, _ = x_sample.shape
    past = cache_sb_k.shape[2]
    tm = 512
    tm_proj = 1024
    m_s = dec_batch * dec_seq
    assert (batch * seq) % tm_proj == 0 and seq % tm == 0 and m_s == tm

    xp = x_prompt.reshape(batch * seq, D_MODEL)
    xs = x_sample.reshape(m_s, D_MODEL)
    mem = mem_prompt.reshape(batch * N_MEM, D_MODEL)
    cache_k = cache_sb_k.reshape(-1, SB_HEAD_DIM)
    cache_v = cache_sb_v.reshape(-1, SB_HEAD_DIM)
    vec = lambda a: a.reshape(1, -1)
    mem_outs = [[] for _ in range(2)]
    prev_p, prev_s = (), ()
    for l in range(depth):
        w_in_l = w_in[l].astype(BF16)
        w_branch_l = w_branch[l].astype(BF16)
        w_out_l = w_out[l].astype(BF16)
        w_ffn_in_l = w_ffn_in[l].astype(BF16)
        w_ffn_out_l = w_ffn_out[l].astype(BF16)
        proj_args = (vec(g_pre_mix[l]), w_in_l, vec(b_gate[l]), vec(ln_sg_g[l]), vec(ln_sg_b[l]))
        ffn_args = (vec(g_pre_ffn[l]), w_ffn_in_l, w_ffn_out_l, vec(g_post_ffn[l]))

        kv32, kv16 = _mem_kv(mem, vec(g_mem[l]), w_mem_kv[l].astype(BF16))
        proj, *prev_p = _in_proj(xp, *proj_args, prev_p, layer=l, depth=depth, tm=tm_proj, emit_vn=False)
        o_sb = _sb_prompt(proj, batch=batch, seq=seq)
        xp = _mixer(xp, o_sb, proj, w_spatial[l], b_spatial[l].T,
                    kv16[0].reshape(batch, N_MEM, D_MODEL), kv16[1].reshape(batch, N_MEM, D_MODEL),
                    w_branch_l, w_out_l, vec(g_post_mix[l]),
                    tm=tm, chunk=SG_CHUNK, seg=tm, rows_per_mem=seq)
        xp = _ffn(xp, *ffn_args, tm=tm)
        mem_outs[0].append(kv32[0].reshape(batch, N_MEM, N_MEM_HEADS, MEM_HEAD_DIM))
        mem_outs[1].append(kv32[1].reshape(batch, N_MEM, N_MEM_HEADS, MEM_HEAD_DIM))

        proj, *prev_s = _in_proj(xs, *proj_args, prev_s, layer=l, depth=depth, tm=tm, emit_vn=True)
        o_sb = _sb_sample(proj, cache_k, cache_v, layer=l, batch=dec_batch, seq=dec_seq, past=past)
        xs = _mixer(xs, o_sb, proj, w_spatial[l][:, :dec_seq, :dec_seq], b_spatial[l][:, :dec_seq].T,
                    cache_mem_k[l].reshape(dec_batch, N_MEM, D_MODEL).astype(BF16),
                    cache_mem_v[l].reshape(dec_batch, N_MEM, D_MODEL).astype(BF16),
                    w_branch_l, w_out_l, vec(g_post_mix[l]),
                    tm=tm, chunk=dec_seq, seg=dec_seq, rows_per_mem=dec_seq)
        xs = _ffn(xs, *ffn_args, tm=tm)

    heads = (N_SB_HEADS, SB_HEAD_DIM)
    return (xp.reshape(batch, seq, D_MODEL), xs.reshape(dec_batch, dec_seq, D_MODEL),
            prev_p[0].reshape(depth, batch, seq, *heads), prev_p[1].reshape(depth, batch, seq, *heads),
            jnp.stack(mem_outs[0]), jnp.stack(mem_outs[1]),
            prev_s[0].reshape(depth, dec_batch, dec_seq, *heads),
            prev_s[1].reshape(depth, dec_batch, dec_seq, *heads),
            prev_s[2].reshape(depth, dec_batch, dec_seq, D_MODEL))
```

```python
import functools

import jax
import jax.numpy as jnp
from jax import lax
from jax.experimental import pallas as pl
from jax.experimental.pallas import tpu as pltpu

F32 = jnp.float32
BF16 = jnp.bfloat16

D_MODEL = 1024
CHUNK_COLS = 1024
N_SB_HEADS = 8
SB_HEAD_DIM = 128
N_SG_GROUPS = 4
SG_GROUP_DIM = 256
SG_CHUNK = 128
N_MEM = 256
N_MEM_HEADS = 4
MEM_HEAD_DIM = 256
N_BRANCH = 3
D_FF = 2816
EPS = 1e-6

COL_Q, COL_K, COL_V, COL_U, COL_VSG, COL_QMEM, COL_GATE = 0, 1, 2, 3, 4, 5, 6
N_COL_BLOCKS = COL_GATE + N_BRANCH

SB_TILE = 128
SB_WINDOW = 3
SB_CHAINS = 4
SB_LOG2_WEIGHT_FLOOR = -150.04
FF_CHUNK = 256
LOG2_E = 1.4426950408889634

VMEM_LIMIT = 56 * 1024 * 1024


def _dot(a, b):
    return jnp.dot(a, b, preferred_element_type=F32)


def _dot_nt(a, b):
    return lax.dot_general(a, b, (((1,), (1,)), ((), ())), preferred_element_type=F32)


def _rmsnorm(x, g):
    return x * lax.rsqrt(jnp.mean(x * x, axis=-1, keepdims=True) + EPS) * g


def _gelu(x):
    return 0.5 * x * (1.0 + jnp.tanh(0.7978845608028654 * (x + 0.044715 * (x * x * x))))


def _sigmoid(x):
    return 1.0 / (1.0 + jnp.exp(-x))


def _params(*semantics):
    return pltpu.CompilerParams(dimension_semantics=semantics, vmem_limit_bytes=VMEM_LIMIT)


def _store_heads(dst_ref, acc):
    tm = acc.shape[0]
    for h in range(N_SB_HEADS):
        dst_ref[pl.ds(h, tm, stride=N_SB_HEADS), :] = acc[:, h * SB_HEAD_DIM:(h + 1) * SB_HEAD_DIM]


def _in_proj_kernel(*refs, n_prev):
    (x_ref, g_ref, w_ref, bg_ref, lng_ref, lnb_ref), refs = refs[:6], refs[6 + n_prev:]
    proj_ref, k_ref, v_ref, vn_ref = refs
    h = _rmsnorm(x_ref[...], g_ref[...]).astype(BF16)
    for j in range(N_COL_BLOCKS):
        cols = slice(j * CHUNK_COLS, (j + 1) * CHUNK_COLS)
        acc = _dot(h, w_ref[:, cols])
        if j == COL_K:
            _store_heads(k_ref, acc)
        elif j == COL_V:
            _store_heads(v_ref, acc)
        elif j == COL_U:
            acc = _gelu(acc)
        elif j == COL_VSG:
            a = _gelu(acc)
            mu = jnp.mean(a, axis=-1, keepdims=True)
            d = a - mu
            var = jnp.mean(d * d, axis=-1, keepdims=True)
            acc = d * lax.rsqrt(var + EPS) * lng_ref[...] + lnb_ref[...]
            if vn_ref is not None:
                vn_ref[...] = acc
        elif j >= COL_GATE:
            g = slice((j - COL_GATE) * CHUNK_COLS, (j - COL_GATE + 1) * CHUNK_COLS)
            acc = _sigmoid(acc + bg_ref[:, g])
        proj_ref[:, cols] = acc.astype(BF16)


def _in_proj(x, g_pre, w_in, b_gate, ln_g, ln_b, prev, *, layer, depth, tm, emit_vn):
    m = x.shape[0]
    n_i = m // tm
    d_in = N_COL_BLOCKS * CHUNK_COLS
    row = lambda i: (i, 0)
    layer_row = lambda i: (layer * n_i + i, 0)
    resident = lambda shape: pl.BlockSpec(shape, lambda i: (0, 0), pipeline_mode=pl.Buffered(1))
    out_shape = [jax.ShapeDtypeStruct((m, d_in), BF16),
                 jax.ShapeDtypeStruct((depth * m * N_SB_HEADS, SB_HEAD_DIM), F32),
                 jax.ShapeDtypeStruct((depth * m * N_SB_HEADS, SB_HEAD_DIM), F32)]
    out_specs = [pl.BlockSpec((tm, d_in), row),
                 pl.BlockSpec((tm * N_SB_HEADS, SB_HEAD_DIM), layer_row),
                 pl.BlockSpec((tm * N_SB_HEADS, SB_HEAD_DIM), layer_row)]
    if emit_vn:
        out_shape.append(jax.ShapeDtypeStruct((depth * m, CHUNK_COLS), F32))
        out_specs.append(pl.BlockSpec((tm, CHUNK_COLS), layer_row))
    if not prev:
        prev = [jnp.zeros(s.shape, s.dtype) for s in out_shape[1:]]
    n_prev = len(prev)
    assert n_prev == len(out_shape) - 1

    def body(*refs):
        if not emit_vn:
            refs = refs + (None,)
        _in_proj_kernel(*refs, n_prev=n_prev)

    return pl.pallas_call(
        body,
        grid=(n_i,),
        in_specs=[pl.BlockSpec((tm, D_MODEL), row),
                  resident((1, D_MODEL)),
                  resident((D_MODEL, d_in)),
                  resident((1, N_BRANCH * CHUNK_COLS)),
                  resident((1, CHUNK_COLS)),
                  resident((1, CHUNK_COLS))]
                 + [pl.BlockSpec(memory_space=pl.ANY)] * n_prev,
        out_specs=out_specs,
        out_shape=out_shape,
        input_output_aliases={6 + p: 1 + p for p in range(n_prev)},
        compiler_params=_params("arbitrary"),
        name="in_proj",
    )(x, g_pre, w_in, b_gate, ln_g, ln_b, *prev)


def _suffix_matrix():
    t = SB_TILE
    j = lax.broadcasted_iota(jnp.int32, (2 * t, 2 * t), 0) % t
    s = lax.broadcasted_iota(jnp.int32, (2 * t, 2 * t), 1)
    return jnp.where((j > s) | (s >= t), 1.0, 0.0).astype(BF16)


def _sb_front(qk, mask):
    zns = [_dot_nt(q, k) * (-(SB_HEAD_DIM ** -0.5) * LOG2_E) for q, k, _ in qk]
    keeps = [jnp.minimum(zn, 0.0) - jnp.log2(1.0 + jnp.exp2(-jnp.abs(zn))) for zn in zns]
    betas = [keep - zn for keep, zn in zip(keeps, zns)]
    pieces = []
    for keep, (_, _, n_tiles) in zip(keeps, qk):
        per_tile = []
        for d in range(n_tiles):
            log_keep = keep[:, d * SB_TILE:(d + 1) * SB_TILE]
            if mask is not None and d == n_tiles - 1:
                log_keep = jnp.where(mask, log_keep, 0.0)
            hi = log_keep.astype(BF16)
            lo = (log_keep - hi.astype(F32)).astype(BF16)
            per_tile.append(jnp.concatenate([hi, lo], axis=1))
        pieces.append(per_tile)
    return betas, pieces


def _sb_back(betas, pieces, vs, carries, mask, u):
    sums = [[_dot(p, u) for p in per_tile] for per_tile in pieces]
    ws = []
    for beta, per_tile, carry in zip(betas, sums, carries):
        n_tiles = len(per_tile)
        w_tiles = [None] * n_tiles
        for d in reversed(range(n_tiles)):
            suffix, total = per_tile[d][:, :SB_TILE], per_tile[d][:, SB_TILE:]
            between = suffix if carry is None else carry + suffix
            w = jnp.exp2(beta[:, d * SB_TILE:(d + 1) * SB_TILE] + between)
            if mask is not None and d == n_tiles - 1:
                w = jnp.where(mask, w, 0.0)
            w_tiles[d] = w.astype(BF16)
            carry = total if carry is None else carry + total
        ws.append((jnp.concatenate(w_tiles, axis=1), carry))
    return [(_dot(w, v), carry) for (w, carry), v in zip(ws, vs)]


def _sb_chains(chains, mask, u):
    betas, pieces = _sb_front([(q, k, n) for q, k, _, n, _ in chains], mask)
    return _sb_back(betas, pieces, [c[2] for c in chains], [c[4] for c in chains], mask, u)


def _live(carry):
    return jnp.max(carry) >= SB_LOG2_WEIGHT_FLOOR


def _sb_prompt_kernel(q_ref, k_ref, v_ref, o_ref, acc_ref, carry_ref):
    t = SB_TILE
    u = _suffix_matrix()
    causal = (lax.broadcasted_iota(jnp.int32, (t, t), 1) < lax.broadcasted_iota(jnp.int32, (t, t), 0))

    def rows(tile, n=1):
        start = tile * t
        return pl.ds(start if isinstance(start, int) else pl.multiple_of(start, t), n * t)

    def keys(qt, n_tiles):
        return rows(qt - (n_tiles - 1), n_tiles)

    def finish(slot, qt, n_tiles, live):
        def body(state):
            kb, _ = state
            (pv, carry), = _sb_chains([(q_ref[rows(qt), :], k_ref[rows(kb), :], v_ref[rows(kb), :], 1,
                                        carry_ref[slot])], None, u)
            acc_ref[slot] += pv
            carry_ref[slot] = carry
            return kb - 1, _live(carry)

        lax.while_loop(lambda s: (s[0] >= 0) & s[1], body, (qt - n_tiles, live))
        o_ref[rows(qt), :] = acc_ref[slot].astype(BF16)

    def settle(qt, sizes, results):
        for s, (pv, carry) in enumerate(results):
            acc_ref[s] = pv
            carry_ref[s] = carry
        for s, n in enumerate(sizes):
            finish(s, qt + s, n, _live(results[s][1]))

    def tile_group(qt, sizes):
        settle(qt, sizes, _sb_chains(
            [(q_ref[rows(qt + s), :], k_ref[keys(qt + s, n), :], v_ref[keys(qt + s, n), :], n, None)
             for s, n in enumerate(sizes)], causal, u))

    n_q = q_ref.shape[0] // t
    first = -(-(SB_WINDOW - 1) // SB_CHAINS) * SB_CHAINS
    for qt in range(0, first, SB_CHAINS):
        tile_group(qt, [min(qt + s + 1, SB_WINDOW) for s in range(SB_CHAINS)])

    def step(i, _):
        tile_group(first + SB_CHAINS * i, [SB_WINDOW] * SB_CHAINS)
        return 0

    lax.fori_loop(0, (n_q - first) // SB_CHAINS, step, 0)


def _sb_prompt(proj, *, batch, seq):
    assert seq % (SB_CHAINS * SB_TILE) == 0
    head = lambda c: pl.BlockSpec((seq, SB_HEAD_DIM), lambda b, h: (b, c * N_SB_HEADS + h))
    return pl.pallas_call(
        _sb_prompt_kernel,
        grid=(batch, N_SB_HEADS),
        in_specs=[head(COL_Q), head(COL_K), head(COL_V)],
        out_specs=pl.BlockSpec((seq, SB_HEAD_DIM), lambda b, h: (b, h)),
        out_shape=jax.ShapeDtypeStruct((batch * seq, N_SB_HEADS * SB_HEAD_DIM), BF16),
        scratch_shapes=[pltpu.VMEM((SB_CHAINS, SB_TILE, SB_HEAD_DIM), F32),
                        pltpu.VMEM((SB_CHAINS, SB_TILE, SB_TILE), F32)],
        compiler_params=_params("arbitrary", "arbitrary"),
        name="sb_prompt",
    )(proj, proj, proj)


def _sb_sample_kernel(q_ref, k_ref, v_ref, ck_hbm, cv_hbm, o_ref, kbuf, vbuf, sem, acc_ref, carry_ref,
                      *, cache_row0, past):
    t = SB_TILE
    n_new = q_ref.shape[0]
    n_blocks = past // t
    block_rows = t * N_SB_HEADS
    u = _suffix_matrix()
    base = cache_row0 + pl.program_id(0) * past * N_SB_HEADS

    def copies(kb, slot):
        src = pl.ds(base + kb * block_rows, block_rows)
        return (pltpu.make_async_copy(ck_hbm.at[src, :], kbuf.at[slot], sem.at[0, slot]),
                pltpu.make_async_copy(cv_hbm.at[src, :], vbuf.at[slot], sem.at[1, slot]))

    def start(kb, slot):
        for c in copies(kb, slot):
            c.start()

    def wait(kb, slot):
        for c in copies(kb, slot):
            c.wait()

    start(n_blocks - 1, 0)

    causal = (lax.broadcasted_iota(jnp.int32, (n_new, t), 1) < lax.broadcasted_iota(jnp.int32, (n_new, t), 0))
    pad = jnp.zeros((t - n_new, SB_HEAD_DIM), BF16)
    head_cols = [slice(h * SB_HEAD_DIM, (h + 1) * SB_HEAD_DIM) for h in range(N_SB_HEADS)]
    results = _sb_chains([(q_ref[:, cols], jnp.concatenate([k_ref[:, cols], pad], axis=0),
                           jnp.concatenate([v_ref[:, cols], pad], axis=0), 1, None)
                          for cols in head_cols], causal, u)
    live = False
    for h, (pv, carry) in enumerate(results):
        acc_ref[h] = pv
        carry_ref[h] = carry
        live = live | _live(carry)

    def body(state):
        kb, _ = state
        slot = (n_blocks - 1 - kb) % 2
        wait(kb, slot)

        @pl.when(kb > 0)
        def _():
            start(kb - 1, 1 - slot)

        head_rows = [pl.ds(h, t, stride=N_SB_HEADS) for h in range(N_SB_HEADS)]
        results = _sb_chains([(q_ref[:, head_cols[h]], kbuf[slot, head_rows[h], :].astype(BF16),
                               vbuf[slot, head_rows[h], :].astype(BF16), 1, carry_ref[h])
                              for h in range(N_SB_HEADS)], None, u)
        live = False
        for h, (pv, carry) in enumerate(results):
            acc_ref[h] += pv
            carry_ref[h] = carry
            live = live | _live(carry)
        return kb - 1, live

    kb_end, _ = lax.while_loop(lambda s: (s[0] >= 0) & s[1], body, (n_blocks - 1, live))

    @pl.when(kb_end >= 0)
    def _():
        wait(kb_end, (n_blocks - 1 - kb_end) % 2)

    for h in range(N_SB_HEADS):
        o_ref[:, h * SB_HEAD_DIM:(h + 1) * SB_HEAD_DIM] = acc_ref[h].astype(BF16)


def _sb_sample(proj, cache_k, cache_v, *, layer, batch, seq, past):
    assert past % SB_TILE == 0 and seq <= SB_TILE
    blk = lambda c: pl.BlockSpec((seq, CHUNK_COLS), lambda b: (b, c))
    block_rows = SB_TILE * N_SB_HEADS
    return pl.pallas_call(
        functools.partial(_sb_sample_kernel, cache_row0=layer * batch * past * N_SB_HEADS, past=past),
        grid=(batch,),
        in_specs=[blk(COL_Q), blk(COL_K), blk(COL_V),
                  pl.BlockSpec(memory_space=pl.ANY), pl.BlockSpec(memory_space=pl.ANY)],
        out_specs=pl.BlockSpec((seq, N_SB_HEADS * SB_HEAD_DIM), lambda b: (b, 0)),
        out_shape=jax.ShapeDtypeStruct((batch * seq, N_SB_HEADS * SB_HEAD_DIM), BF16),
        scratch_shapes=[pltpu.VMEM((2, block_rows, SB_HEAD_DIM), F32),
                        pltpu.VMEM((2, block_rows, SB_HEAD_DIM), F32),
                        pltpu.SemaphoreType.DMA((2, 2)),
                        pltpu.VMEM((N_SB_HEADS, seq, SB_HEAD_DIM), F32),
                        pltpu.VMEM((N_SB_HEADS, seq, SB_TILE), F32)],
        compiler_params=_params("arbitrary"),
        name="sb_sample",
    )(proj, proj, proj, cache_k, cache_v)


def _mixer_kernel(x_ref, osb_ref, u_ref, vn_ref, qm_ref, g0_ref, g1_ref, g2_ref, ws_ref, bs_ref,
                  mk_ref, mv_ref, wb_ref, wo_ref, gp_ref, y_ref, osg_ref, omem_ref, *, chunk, seg):
    tm = x_ref.shape[0]
    r = lax.broadcasted_iota(jnp.int32, (chunk, chunk), 0)
    c = lax.broadcasted_iota(jnp.int32, (chunk, chunk), 1)
    for g in range(N_SG_GROUPS):
        w_g = jnp.where(c <= r, ws_ref[g], 0.0).astype(BF16)
        b_g = bs_ref[:, g:g + 1]
        cols = slice(g * SG_GROUP_DIM, (g + 1) * SG_GROUP_DIM)
        for n in range(tm // chunk):
            rows = slice(n * chunk, (n + 1) * chunk)
            sg = _dot(w_g, vn_ref[rows, cols]) + b_g
            osg_ref[rows, cols] = (u_ref[rows, cols].astype(F32) * sg).astype(BF16)
    for s in range(tm // seg):
        rows = slice(s * seg, (s + 1) * seg)
        for h in range(N_MEM_HEADS):
            cols = slice(h * MEM_HEAD_DIM, (h + 1) * MEM_HEAD_DIM)
            sc = _dot_nt(qm_ref[rows, cols], mk_ref[s, :, cols]) * (MEM_HEAD_DIM ** -0.5)
            p = jnp.exp(sc - jnp.max(sc, axis=-1, keepdims=True))
            o = _dot(p.astype(BF16), mv_ref[s, :, cols]) / jnp.sum(p, axis=-1, keepdims=True)
            omem_ref[rows, cols] = o.astype(BF16)
    merged = (g0_ref[...].astype(F32) * _dot(osb_ref[...], wb_ref[0])
              + g1_ref[...].astype(F32) * _dot(osg_ref[...], wb_ref[1])
              + g2_ref[...].astype(F32) * _dot(omem_ref[...], wb_ref[2]))
    out = _dot(merged.astype(BF16), wo_ref[...])
    y_ref[...] = x_ref[...] + _rmsnorm(out, gp_ref[...])


def _mixer(x, o_sb, proj, w_s, b_s_t, mem_k, mem_v, w_branch, w_out, g_post, *, tm, chunk, seg,
           rows_per_mem):
    m = x.shape[0]
    row = lambda i: (i, 0)
    col = lambda cb: pl.BlockSpec((tm, CHUNK_COLS), lambda i: (i, cb))
    mem_blocks = tm // seg
    mem = pl.BlockSpec((mem_blocks, N_MEM, D_MODEL), lambda i: (i * tm // rows_per_mem // mem_blocks, 0, 0))
    const2 = lambda i: (0, 0)
    const3 = lambda i: (0, 0, 0)
    return pl.pallas_call(
        functools.partial(_mixer_kernel, chunk=chunk, seg=seg),
        grid=(m // tm,),
        in_specs=[pl.BlockSpec((tm, D_MODEL), row), pl.BlockSpec((tm, D_MODEL), row),
                  col(COL_U), col(COL_VSG), col(COL_QMEM), col(COL_GATE), col(COL_GATE + 1),
                  col(COL_GATE + 2),
                  pl.BlockSpec((N_SG_GROUPS, chunk, chunk), const3),
                  pl.BlockSpec((chunk, N_SG_GROUPS), const2),
                  mem, mem,
                  pl.BlockSpec((N_BRANCH, D_MODEL, D_MODEL), const3),
                  pl.BlockSpec((D_MODEL, D_MODEL), const2),
                  pl.BlockSpec((1, D_MODEL), const2)],
        out_specs=pl.BlockSpec((tm, D_MODEL), row),
        out_shape=jax.ShapeDtypeStruct((m, D_MODEL), F32),
        scratch_shapes=[pltpu.VMEM((tm, D_MODEL), BF16), pltpu.VMEM((tm, D_MODEL), BF16)],
        compiler_params=_params("arbitrary"),
        name="mixer",
    )(x, o_sb, proj, proj, proj, proj, proj, proj, w_s, b_s_t, mem_k, mem_v, w_branch, w_out, g_post)


def _ffn_kernel(x_ref, gpre_ref, wi_ref, wo_ref, gpost_ref, y_ref, act_ref):
    h = _rmsnorm(x_ref[...], gpre_ref[...]).astype(BF16)
    for c in range(0, D_FF, FF_CHUNK):
        a = _dot(h, wi_ref[:, c:c + FF_CHUNK])
        b = _dot(h, wi_ref[:, D_FF + c:D_FF + c + FF_CHUNK])
        act_ref[:, c:c + FF_CHUNK] = (a * _sigmoid(a) * b).astype(BF16)
    out = _dot(act_ref[...], wo_ref[...])
    y_ref[...] = x_ref[...] + _rmsnorm(out, gpost_ref[...])


def _ffn(x, g_pre, w_ffn_in, w_ffn_out, g_post, *, tm):
    m = x.shape[0]
    row = lambda i: (i, 0)
    resident = lambda shape: pl.BlockSpec(shape, lambda i: (0, 0), pipeline_mode=pl.Buffered(1))
    return pl.pallas_call(
        _ffn_kernel,
        grid=(m // tm,),
        in_specs=[pl.BlockSpec((tm, D_MODEL), row),
                  resident((1, D_MODEL)),
                  resident((D_MODEL, 2 * D_FF)),
                  resident((D_FF, D_MODEL)),
                  resident((1, D_MODEL))],
        out_specs=pl.BlockSpec((tm, D_MODEL), row),
        out_shape=jax.ShapeDtypeStruct((m, D_MODEL), F32),
        scratch_shapes=[pltpu.VMEM((tm, D_FF), BF16)],
        compiler_params=_params("arbitrary"),
        name="ffn",
    )(x, g_pre, w_ffn_in, w_ffn_out, g_post)


def _mem_kv_kernel(mem_ref, g_ref, w_ref, kv32_ref, kv16_ref):
    kv = _dot(_rmsnorm(mem_ref[...], g_ref[...]).astype(BF16), w_ref[...])
    kv32_ref[0] = kv
    kv16_ref[0] = kv.astype(BF16)


def _mem_kv(mem, g_mem, w_mem_kv):
    m = mem.shape[0]
    out = pl.BlockSpec((1, m, D_MODEL), lambda j: (j, 0, 0))
    return pl.pallas_call(
        _mem_kv_kernel,
        grid=(2,),
        in_specs=[pl.BlockSpec((m, D_MODEL), lambda j: (0, 0)),
                  pl.BlockSpec((1, D_MODEL), lambda j: (0, 0)),
                  pl.BlockSpec((D_MODEL, D_MODEL), lambda j: (0, j))],
        out_specs=[out, out],
        out_shape=[jax.ShapeDtypeStruct((2, m, D_MODEL), F32),
                   jax.ShapeDtypeStruct((2, m, D_MODEL), BF16)],
        compiler_params=_params("arbitrary"),
        name="mem_kv",
    )(mem, g_mem, w_mem_kv)


def kernel(x_prompt, x_sample, cache_sb_k, cache_sb_v, cache_mem_k, cache_mem_v, mem_prompt,
           g_pre_mix, w_in, b_gate, ln_sg_g, ln_sg_b, w_spatial, b_spatial, g_mem, w_mem_kv,
           w_branch, w_out, g_post_mix, g_pre_ffn, w_ffn_in, w_ffn_out, g_post_ffn):
    depth = w_in.shape[0]
    batch, seq, _ = x_prompt.shape
    dec_batch, dec_seq, _ = x_sample.shape
    past = cache_sb_k.shape[2]
    tm = 512
    tm_proj = 256
    m_s = dec_batch * dec_seq
    assert (batch * seq) % tm_proj == 0 and seq % tm == 0 and m_s == tm

    xp = x_prompt.reshape(batch * seq, D_MODEL)
    xs = x_sample.reshape(m_s, D_MODEL)
    mem = mem_prompt.reshape(batch * N_MEM, D_MODEL)
    cache_k = cache_sb_k.reshape(-1, SB_HEAD_DIM)
    cache_v = cache_sb_v.reshape(-1, SB_HEAD_DIM)
    vec = lambda a: a.reshape(1, -1)
    mem_outs = [[] for _ in range(2)]
    prev_p, prev_s = (), ()
    for l in range(depth):
        w_in_l = w_in[l].astype(BF16)
        w_branch_l = w_branch[l].astype(BF16)
        w_out_l = w_out[l].astype(BF16)
        w_ffn_in_l = w_ffn_in[l].astype(BF16)
        w_ffn_out_l = w_ffn_out[l].astype(BF16)
        proj_args = (vec(g_pre_mix[l]), w_in_l, vec(b_gate[l]), vec(ln_sg_g[l]), vec(ln_sg_b[l]))
        ffn_args = (vec(g_pre_ffn[l]), w_ffn_in_l, w_ffn_out_l, vec(g_post_ffn[l]))

        kv32, kv16 = _mem_kv(mem, vec(g_mem[l]), w_mem_kv[l].astype(BF16))
        proj, *prev_p = _in_proj(xp, *proj_args, prev_p, layer=l, depth=depth, tm=tm_proj, emit_vn=False)
        o_sb = _sb_prompt(proj, batch=batch, seq=seq)
        xp = _mixer(xp, o_sb, proj, w_spatial[l], b_spatial[l].T,
                    kv16[0].reshape(batch, N_MEM, D_MODEL), kv16[1].reshape(batch, N_MEM, D_MODEL),
                    w_branch_l, w_out_l, vec(g_post_mix[l]),
                    tm=tm, chunk=SG_CHUNK, seg=tm, rows_per_mem=seq)
        xp = _ffn(xp, *ffn_args, tm=tm)
        mem_outs[0].append(kv32[0].reshape(batch, N_MEM, N_MEM_HEADS, MEM_HEAD_DIM))
        mem_outs[1].append(kv32[1].reshape(batch, N_MEM, N_MEM_HEADS, MEM_HEAD_DIM))

        proj, *prev_s = _in_proj(xs, *proj_args, prev_s, layer=l, depth=depth, tm=tm_proj, emit_vn=True)
        o_sb = _sb_sample(proj, cache_k, cache_v, layer=l, batch=dec_batch, seq=dec_seq, past=past)
        xs = _mixer(xs, o_sb, proj, w_spatial[l][:, :dec_seq, :dec_seq], b_spatial[l][:, :dec_seq].T,
                    cache_mem_k[l].reshape(dec_batch, N_MEM, D_MODEL).astype(BF16),
                    cache_mem_v[l].reshape(dec_batch, N_MEM, D_MODEL).astype(BF16),
                    w_branch_l, w_out_l, vec(g_post_mix[l]),
                    tm=tm, chunk=dec_seq, seg=dec_seq, rows_per_mem=dec_seq)
        xs = _ffn(xs, *ffn_args, tm=tm)

    heads = (N_SB_HEADS, SB_HEAD_DIM)
    return (xp.reshape(batch, seq, D_MODEL), xs.reshape(dec_batch, dec_seq, D_MODEL),
            prev_p[0].reshape(depth, batch, seq, *heads), prev_p[1].reshape(depth, batch, seq, *heads),
            jnp.stack(mem_outs[0]), jnp.stack(mem_outs[1]),
            prev_s[0].reshape(depth, dec_batch, dec_seq, *heads),
            prev_s[1].reshape(depth, dec_batch, dec_seq, *heads),
            prev_s[2].reshape(depth, dec_batch, dec_seq, D_MODEL))
```

```python
import functools

import jax
import jax.numpy as jnp
from jax import lax
from jax.experimental import pallas as pl
from jax.experimental.pallas import tpu as pltpu

F32 = jnp.float32
BF16 = jnp.bfloat16

D_MODEL = 1024
CHUNK_COLS = 1024
N_SB_HEADS = 8
SB_HEAD_DIM = 128
N_SG_GROUPS = 4
SG_GROUP_DIM = 256
SG_CHUNK = 128
N_MEM = 256
N_MEM_HEADS = 4
MEM_HEAD_DIM = 256
N_BRANCH = 3
D_FF = 2816
EPS = 1e-6

COL_Q, COL_K, COL_V, COL_U, COL_VSG, COL_QMEM, COL_GATE = 0, 1, 2, 3, 4, 5, 6
N_COL_BLOCKS = COL_GATE + N_BRANCH

SB_TILE = 128
SB_WINDOW = 3
SB_CHAINS = 8
SB_LOG2_WEIGHT_FLOOR = -150.04
FF_CHUNK = 256
LOG2_E = 1.4426950408889634

VMEM_LIMIT = 56 * 1024 * 1024


def _dot(a, b):
    return jnp.dot(a, b, preferred_element_type=F32)


def _dot_nt(a, b):
    return lax.dot_general(a, b, (((1,), (1,)), ((), ())), preferred_element_type=F32)


def _rmsnorm(x, g):
    return x * lax.rsqrt(jnp.mean(x * x, axis=-1, keepdims=True) + EPS) * g


def _gelu(x):
    return 0.5 * x * (1.0 + jnp.tanh(0.7978845608028654 * (x + 0.044715 * (x * x * x))))


def _sigmoid(x):
    return 1.0 / (1.0 + jnp.exp(-x))


def _params(*semantics):
    return pltpu.CompilerParams(dimension_semantics=semantics, vmem_limit_bytes=VMEM_LIMIT)


def _store_heads(dst_ref, acc):
    tm = acc.shape[0]
    for r in range(dst_ref.shape[0]):
        for h in range(N_SB_HEADS):
            dst_ref[r, pl.ds(h, tm, stride=N_SB_HEADS), :] = acc[:, h * SB_HEAD_DIM:(h + 1) * SB_HEAD_DIM]


def _in_proj_kernel(*refs, n_prev):
    (x_ref, g_ref, w_ref, bg_ref, lng_ref, lnb_ref), refs = refs[:6], refs[6 + n_prev:]
    proj_ref, k_ref, v_ref, vn_ref = refs
    h = _rmsnorm(x_ref[...], g_ref[...]).astype(BF16)
    for j in range(N_COL_BLOCKS):
        cols = slice(j * CHUNK_COLS, (j + 1) * CHUNK_COLS)
        acc = _dot(h, w_ref[:, cols])
        if j == COL_K:
            _store_heads(k_ref, acc)
        elif j == COL_V:
            _store_heads(v_ref, acc)
        elif j == COL_U:
            acc = _gelu(acc)
        elif j == COL_VSG:
            a = _gelu(acc)
            mu = jnp.mean(a, axis=-1, keepdims=True)
            d = a - mu
            var = jnp.mean(d * d, axis=-1, keepdims=True)
            acc = d * lax.rsqrt(var + EPS) * lng_ref[...] + lnb_ref[...]
            if vn_ref is not None:
                for r in range(vn_ref.shape[0]):
                    vn_ref[r] = acc
        elif j >= COL_GATE:
            g = slice((j - COL_GATE) * CHUNK_COLS, (j - COL_GATE + 1) * CHUNK_COLS)
            acc = _sigmoid(acc + bg_ref[:, g])
        proj_ref[:, cols] = acc.astype(BF16)


def _in_proj(x, g_pre, w_in, b_gate, ln_g, ln_b, prev, *, layer, depth, tm, emit_vn):
    m = x.shape[0]
    n_i = m // tm
    d_in = N_COL_BLOCKS * CHUNK_COLS
    row = lambda i: (i, 0)
    resident = lambda shape: pl.BlockSpec(shape, lambda i: (0, 0), pipeline_mode=pl.Buffered(1))
    layers = depth if layer == 0 else 1
    layer_rows = lambda shape: pl.BlockSpec((layers, *shape), lambda i: (layer, i, 0))
    out_shape = [jax.ShapeDtypeStruct((m, d_in), BF16),
                 jax.ShapeDtypeStruct((depth, m * N_SB_HEADS, SB_HEAD_DIM), F32),
                 jax.ShapeDtypeStruct((depth, m * N_SB_HEADS, SB_HEAD_DIM), F32)]
    out_specs = [pl.BlockSpec((tm, d_in), row),
                 layer_rows((tm * N_SB_HEADS, SB_HEAD_DIM)),
                 layer_rows((tm * N_SB_HEADS, SB_HEAD_DIM))]
    if emit_vn:
        out_shape.append(jax.ShapeDtypeStruct((depth, m, CHUNK_COLS), F32))
        out_specs.append(layer_rows((tm, CHUNK_COLS)))
    n_prev = len(prev)
    assert n_prev == (0 if layer == 0 else len(out_shape) - 1)

    def body(*refs):
        if not emit_vn:
            refs = refs + (None,)
        _in_proj_kernel(*refs, n_prev=n_prev)

    return pl.pallas_call(
        body,
        grid=(n_i,),
        in_specs=[pl.BlockSpec((tm, D_MODEL), row),
                  resident((1, D_MODEL)),
                  resident((D_MODEL, d_in)),
                  resident((1, N_BRANCH * CHUNK_COLS)),
                  resident((1, CHUNK_COLS)),
                  resident((1, CHUNK_COLS))]
                 + [pl.BlockSpec(memory_space=pl.ANY)] * n_prev,
        out_specs=out_specs,
        out_shape=out_shape,
        input_output_aliases={6 + p: 1 + p for p in range(n_prev)},
        compiler_params=_params("arbitrary"),
        name="in_proj",
    )(x, g_pre, w_in, b_gate, ln_g, ln_b, *prev)


def _suffix_matrix():
    t = SB_TILE
    j = lax.broadcasted_iota(jnp.int32, (2 * t, 2 * t), 0) % t
    s = lax.broadcasted_iota(jnp.int32, (2 * t, 2 * t), 1)
    return jnp.where((j > s) | (s >= t), 1.0, 0.0).astype(BF16)


def _sb_front(qk, mask):
    zns = [_dot_nt(q, k) * (-(SB_HEAD_DIM ** -0.5) * LOG2_E) for q, k, _ in qk]
    keeps = [jnp.minimum(zn, 0.0) - jnp.log2(1.0 + jnp.exp2(-jnp.abs(zn))) for zn in zns]
    betas = [keep - zn for keep, zn in zip(keeps, zns)]
    pieces = []
    for keep, (_, _, n_tiles) in zip(keeps, qk):
        per_tile = []
        for d in range(n_tiles):
            log_keep = keep[:, d * SB_TILE:(d + 1) * SB_TILE]
            if mask is not None and d == n_tiles - 1:
                log_keep = jnp.where(mask, log_keep, 0.0)
            hi = log_keep.astype(BF16)
            lo = (log_keep - hi.astype(F32)).astype(BF16)
            per_tile.append(jnp.concatenate([hi, lo], axis=1))
        pieces.append(per_tile)
    return betas, pieces


def _sb_back(betas, pieces, vs, carries, mask, u):
    sums = [[_dot(p, u) for p in per_tile] for per_tile in pieces]
    ws = []
    for beta, per_tile, carry in zip(betas, sums, carries):
        n_tiles = len(per_tile)
        w_tiles = [None] * n_tiles
        for d in reversed(range(n_tiles)):
            suffix, total = per_tile[d][:, :SB_TILE], per_tile[d][:, SB_TILE:]
            between = suffix if carry is None else carry + suffix
            w = jnp.exp2(beta[:, d * SB_TILE:(d + 1) * SB_TILE] + between)
            if mask is not None and d == n_tiles - 1:
                w = jnp.where(mask, w, 0.0)
            w_tiles[d] = w.astype(BF16)
            carry = total if carry is None else carry + total
        ws.append((jnp.concatenate(w_tiles, axis=1), carry))
    return [(_dot(w, v), carry) for (w, carry), v in zip(ws, vs)]


def _sb_chains(chains, mask, u):
    betas, pieces = _sb_front([(q, k, n) for q, k, _, n, _ in chains], mask)
    return _sb_back(betas, pieces, [c[2] for c in chains], [c[4] for c in chains], mask, u)


def _live(carry):
    return jnp.max(carry) >= SB_LOG2_WEIGHT_FLOOR


def _sb_prompt_kernel(q_ref, k_ref, v_ref, o_ref, acc_ref, carry_ref):
    t = SB_TILE
    u = _suffix_matrix()
    causal = (lax.broadcasted_iota(jnp.int32, (t, t), 1) < lax.broadcasted_iota(jnp.int32, (t, t), 0))

    def rows(tile, n=1):
        start = tile * t
        return pl.ds(start if isinstance(start, int) else pl.multiple_of(start, t), n * t)

    def keys(qt, n_tiles):
        return rows(qt - (n_tiles - 1), n_tiles)

    def settle(qt, sizes, results):
        def pending(j, carries):
            worst = None
            for s, n in enumerate(sizes):
                c = jnp.where(qt + s - n - j >= 0, carries[s], -jnp.inf)
                worst = c if worst is None else jnp.maximum(worst, c)
            return _live(worst)

        for s, (pv, carry) in enumerate(results):
            acc_ref[s] = pv
            carry_ref[s] = carry
            o_ref[rows(qt + s), :] = pv.astype(BF16)

        def body(state):
            j, _ = state
            kbs = [qt + s - n - j for s, n in enumerate(sizes)]
            tiles = [rows(jnp.maximum(kb, 0)) for kb in kbs]
            stepped = _sb_chains([(q_ref[rows(qt + s), :], k_ref[tiles[s], :], v_ref[tiles[s], :], 1,
                                   carry_ref[s]) for s in range(len(sizes))], None, u)
            carries = []
            for s, (pv, carry) in enumerate(stepped):
                acc = acc_ref[s] + jnp.where(kbs[s] >= 0, pv, 0.0)
                carry = jnp.where(kbs[s] >= 0, carry, carry_ref[s])
                acc_ref[s] = acc
                carry_ref[s] = carry
                o_ref[rows(qt + s), :] = acc.astype(BF16)
                carries.append(carry)
            return j + 1, pending(j + 1, carries)

        lax.while_loop(lambda state: state[1], body, (0, pending(0, [c for _, c in results])))

    def tile_group(qt, sizes):
        settle(qt, sizes, _sb_chains(
            [(q_ref[rows(qt + s), :], k_ref[keys(qt + s, n), :], v_ref[keys(qt + s, n), :], n, None)
             for s, n in enumerate(sizes)], causal, u))

    n_q = q_ref.shape[0] // t
    first = -(-(SB_WINDOW - 1) // SB_CHAINS) * SB_CHAINS
    for qt in range(0, first, SB_CHAINS):
        tile_group(qt, [min(qt + s + 1, SB_WINDOW) for s in range(SB_CHAINS)])

    def step(i, _):
        tile_group(first + SB_CHAINS * i, [SB_WINDOW] * SB_CHAINS)
        return 0

    lax.fori_loop(0, (n_q - first) // SB_CHAINS, step, 0)


def _sb_prompt(proj, *, batch, seq):
    assert seq % (SB_CHAINS * SB_TILE) == 0
    head = lambda c: pl.BlockSpec((seq, SB_HEAD_DIM), lambda b, h: (b, c * N_SB_HEADS + h))
    return pl.pallas_call(
        _sb_prompt_kernel,
        grid=(batch, N_SB_HEADS),
        in_specs=[head(COL_Q), head(COL_K), head(COL_V)],
        out_specs=pl.BlockSpec((seq, SB_HEAD_DIM), lambda b, h: (b, h)),
        out_shape=jax.ShapeDtypeStruct((batch * seq, N_SB_HEADS * SB_HEAD_DIM), BF16),
        scratch_shapes=[pltpu.VMEM((SB_CHAINS, SB_TILE, SB_HEAD_DIM), F32),
                        pltpu.VMEM((SB_CHAINS, SB_TILE, SB_TILE), F32)],
        compiler_params=_params("arbitrary", "arbitrary"),
        name="sb_prompt",
    )(proj, proj, proj)


def _sb_sample_kernel(q_ref, k_ref, v_ref, ck_hbm, cv_hbm, o_ref, kbuf, vbuf, sem, acc_ref, carry_ref,
                      *, cache_row0, past):
    t = SB_TILE
    n_new = q_ref.shape[0]
    n_blocks = past // t
    block_rows = t * N_SB_HEADS
    u = _suffix_matrix()
    base = cache_row0 + pl.program_id(0) * past * N_SB_HEADS

    def copies(kb, slot):
        src = pl.ds(base + kb * block_rows, block_rows)
        return (pltpu.make_async_copy(ck_hbm.at[src, :], kbuf.at[slot], sem.at[0, slot]),
                pltpu.make_async_copy(cv_hbm.at[src, :], vbuf.at[slot], sem.at[1, slot]))

    def start(kb, slot):
        for c in copies(kb, slot):
            c.start()

    def wait(kb, slot):
        for c in copies(kb, slot):
            c.wait()

    start(n_blocks - 1, 0)

    causal = (lax.broadcasted_iota(jnp.int32, (n_new, t), 1) < lax.broadcasted_iota(jnp.int32, (n_new, t), 0))
    pad = jnp.zeros((t - n_new, SB_HEAD_DIM), BF16)
    head_cols = [slice(h * SB_HEAD_DIM, (h + 1) * SB_HEAD_DIM) for h in range(N_SB_HEADS)]
    results = _sb_chains([(q_ref[:, cols], jnp.concatenate([k_ref[:, cols], pad], axis=0),
                           jnp.concatenate([v_ref[:, cols], pad], axis=0), 1, None)
                          for cols in head_cols], causal, u)
    live = False
    for h, (pv, carry) in enumerate(results):
        acc_ref[h] = pv
        carry_ref[h] = carry
        live = live | _live(carry)

    def body(state):
        kb, _ = state
        slot = (n_blocks - 1 - kb) % 2
        wait(kb, slot)

        @pl.when(kb > 0)
        def _():
            start(kb - 1, 1 - slot)

        head_rows = [pl.ds(h, t, stride=N_SB_HEADS) for h in range(N_SB_HEADS)]
        results = _sb_chains([(q_ref[:, head_cols[h]], kbuf[slot, head_rows[h], :].astype(BF16),
                               vbuf[slot, head_rows[h], :].astype(BF16), 1, carry_ref[h])
                              for h in range(N_SB_HEADS)], None, u)
        live = False
        for h, (pv, carry) in enumerate(results):
            acc_ref[h] += pv
            carry_ref[h] = carry
            live = live | _live(carry)
        return kb - 1, live

    kb_end, _ = lax.while_loop(lambda s: (s[0] >= 0) & s[1], body, (n_blocks - 1, live))

    @pl.when(kb_end >= 0)
    def _():
        wait(kb_end, (n_blocks - 1 - kb_end) % 2)

    for h in range(N_SB_HEADS):
        o_ref[:, h * SB_HEAD_DIM:(h + 1) * SB_HEAD_DIM] = acc_ref[h].astype(BF16)


def _sb_sample(proj, cache_k, cache_v, *, layer, batch, seq, past):
    assert past % SB_TILE == 0 and seq <= SB_TILE
    blk = lambda c: pl.BlockSpec((seq, CHUNK_COLS), lambda b: (b, c))
    block_rows = SB_TILE * N_SB_HEADS
    return pl.pallas_call(
        functools.partial(_sb_sample_kernel, cache_row0=layer * batch * past * N_SB_HEADS, past=past),
        grid=(batch,),
        in_specs=[blk(COL_Q), blk(COL_K), blk(COL_V),
                  pl.BlockSpec(memory_space=pl.ANY), pl.BlockSpec(memory_space=pl.ANY)],
        out_specs=pl.BlockSpec((seq, N_SB_HEADS * SB_HEAD_DIM), lambda b: (b, 0)),
        out_shape=jax.ShapeDtypeStruct((batch * seq, N_SB_HEADS * SB_HEAD_DIM), BF16),
        scratch_shapes=[pltpu.VMEM((2, block_rows, SB_HEAD_DIM), F32),
                        pltpu.VMEM((2, block_rows, SB_HEAD_DIM), F32),
                        pltpu.SemaphoreType.DMA((2, 2)),
                        pltpu.VMEM((N_SB_HEADS, seq, SB_HEAD_DIM), F32),
                        pltpu.VMEM((N_SB_HEADS, seq, SB_TILE), F32)],
        compiler_params=_params("arbitrary"),
        name="sb_sample",
    )(proj, proj, proj, cache_k, cache_v)


def _mixer_kernel(x_ref, osb_ref, u_ref, vn_ref, qm_ref, g0_ref, g1_ref, g2_ref, ws_ref, bs_ref,
                  mk_ref, mv_ref, wb_ref, wo_ref, gp_ref, y_ref, osg_ref, omem_ref, *, chunk, seg):
    tm = x_ref.shape[0]
    r = lax.broadcasted_iota(jnp.int32, (chunk, chunk), 0)
    c = lax.broadcasted_iota(jnp.int32, (chunk, chunk), 1)
    for g in range(N_SG_GROUPS):
        w_g = jnp.where(c <= r, ws_ref[g], 0.0).astype(BF16)
        b_g = bs_ref[:, g:g + 1]
        cols = slice(g * SG_GROUP_DIM, (g + 1) * SG_GROUP_DIM)
        for n in range(tm // chunk):
            rows = slice(n * chunk, (n + 1) * chunk)
            sg = _dot(w_g, vn_ref[rows, cols]) + b_g
            osg_ref[rows, cols] = (u_ref[rows, cols].astype(F32) * sg).astype(BF16)
    for s in range(tm // seg):
        rows = slice(s * seg, (s + 1) * seg)
        for h in range(N_MEM_HEADS):
            cols = slice(h * MEM_HEAD_DIM, (h + 1) * MEM_HEAD_DIM)
            sc = _dot_nt(qm_ref[rows, cols], mk_ref[s, :, cols]) * (MEM_HEAD_DIM ** -0.5)
            p = jnp.exp(sc - jnp.max(sc, axis=-1, keepdims=True))
            o = _dot(p.astype(BF16), mv_ref[s, :, cols]) / jnp.sum(p, axis=-1, keepdims=True)
            omem_ref[rows, cols] = o.astype(BF16)
    merged = (g0_ref[...].astype(F32) * _dot(osb_ref[...], wb_ref[0])
              + g1_ref[...].astype(F32) * _dot(osg_ref[...], wb_ref[1])
              + g2_ref[...].astype(F32) * _dot(omem_ref[...], wb_ref[2]))
    out = _dot(merged.astype(BF16), wo_ref[...])
    y_ref[...] = x_ref[...] + _rmsnorm(out, gp_ref[...])


def _mixer(x, o_sb, proj, w_s, b_s_t, mem_k, mem_v, w_branch, w_out, g_post, *, tm, chunk, seg,
           rows_per_mem):
    m = x.shape[0]
    row = lambda i: (i, 0)
    col = lambda cb: pl.BlockSpec((tm, CHUNK_COLS), lambda i: (i, cb))
    mem_blocks = tm // seg
    mem = pl.BlockSpec((mem_blocks, N_MEM, D_MODEL), lambda i: (i * tm // rows_per_mem // mem_blocks, 0, 0))
    const2 = lambda i: (0, 0)
    const3 = lambda i: (0, 0, 0)
    return pl.pallas_call(
        functools.partial(_mixer_kernel, chunk=chunk, seg=seg),
        grid=(m // tm,),
        in_specs=[pl.BlockSpec((tm, D_MODEL), row), pl.BlockSpec((tm, D_MODEL), row),
                  col(COL_U), col(COL_VSG), col(COL_QMEM), col(COL_GATE), col(COL_GATE + 1),
                  col(COL_GATE + 2),
                  pl.BlockSpec((N_SG_GROUPS, chunk, chunk), const3),
                  pl.BlockSpec((chunk, N_SG_GROUPS), const2),
                  mem, mem,
                  pl.BlockSpec((N_BRANCH, D_MODEL, D_MODEL), const3),
                  pl.BlockSpec((D_MODEL, D_MODEL), const2),
                  pl.BlockSpec((1, D_MODEL), const2)],
        out_specs=pl.BlockSpec((tm, D_MODEL), row),
        out_shape=jax.ShapeDtypeStruct((m, D_MODEL), F32),
        scratch_shapes=[pltpu.VMEM((tm, D_MODEL), BF16), pltpu.VMEM((tm, D_MODEL), BF16)],
        compiler_params=_params("arbitrary"),
        name="mixer",
    )(x, o_sb, proj, proj, proj, proj, proj, proj, w_s, b_s_t, mem_k, mem_v, w_branch, w_out, g_post)


def _ffn_kernel(x_ref, gpre_ref, wi_ref, wo_ref, gpost_ref, y_ref, act_ref):
    h = _rmsnorm(x_ref[...], gpre_ref[...]).astype(BF16)
    for c in range(0, D_FF, FF_CHUNK):
        a = _dot(h, wi_ref[:, c:c + FF_CHUNK])
        b = _dot(h, wi_ref[:, D_FF + c:D_FF + c + FF_CHUNK])
        act_ref[:, c:c + FF_CHUNK] = (a * _sigmoid(a) * b).astype(BF16)
    out = _dot(act_ref[...], wo_ref[...])
    y_ref[...] = x_ref[...] + _rmsnorm(out, gpost_ref[...])


def _ffn(x, g_pre, w_ffn_in, w_ffn_out, g_post, *, tm):
    m = x.shape[0]
    row = lambda i: (i, 0)
    resident = lambda shape: pl.BlockSpec(shape, lambda i: (0, 0), pipeline_mode=pl.Buffered(1))
    return pl.pallas_call(
        _ffn_kernel,
        grid=(m // tm,),
        in_specs=[pl.BlockSpec((tm, D_MODEL), row),
                  resident((1, D_MODEL)),
                  resident((D_MODEL, 2 * D_FF)),
                  resident((D_FF, D_MODEL)),
                  resident((1, D_MODEL))],
        out_specs=pl.BlockSpec((tm, D_MODEL), row),
        out_shape=jax.ShapeDtypeStruct((m, D_MODEL), F32),
        scratch_shapes=[pltpu.VMEM((tm, D_FF), BF16)],
        compiler_params=_params("arbitrary"),
        name="ffn",
    )(x, g_pre, w_ffn_in, w_ffn_out, g_post)


def _mem_kv_kernel(mem_ref, g_ref, w_ref, kv32_ref, kv16_ref):
    kv = _dot(_rmsnorm(mem_ref[...], g_ref[...]).astype(BF16), w_ref[...])
    kv32_ref[0] = kv
    kv16_ref[0] = kv.astype(BF16)


def _mem_kv(mem, g_mem, w_mem_kv):
    m = mem.shape[0]
    out = pl.BlockSpec((1, m, D_MODEL), lambda j: (j, 0, 0))
    return pl.pallas_call(
        _mem_kv_kernel,
        grid=(2,),
        in_specs=[pl.BlockSpec((m, D_MODEL), lambda j: (0, 0)),
                  pl.BlockSpec((1, D_MODEL), lambda j: (0, 0)),
                  pl.BlockSpec((D_MODEL, D_MODEL), lambda j: (0, j))],
        out_specs=[out, out],
        out_shape=[jax.ShapeDtypeStruct((2, m, D_MODEL), F32),
                   jax.ShapeDtypeStruct((2, m, D_MODEL), BF16)],
        compiler_params=_params("arbitrary"),
        name="mem_kv",
    )(mem, g_mem, w_mem_kv)


def kernel(x_prompt, x_sample, cache_sb_k, cache_sb_v, cache_mem_k, cache_mem_v, mem_prompt,
           g_pre_mix, w_in, b_gate, ln_sg_g, ln_sg_b, w_spatial, b_spatial, g_mem, w_mem_kv,
           w_branch, w_out, g_post_mix, g_pre_ffn, w_ffn_in, w_ffn_out, g_post_ffn):
    depth = w_in.shape[0]
    batch, seq, _ = x_prompt.shape
    dec_batch, dec_seq, _ = x_sample.shape
    past = cache_sb_k.shape[2]
    tm = 512
    tm_proj = 256
    m_s = dec_batch * dec_seq
    assert (batch * seq) % tm_proj == 0 and seq % tm == 0 and m_s == tm

    xp = x_prompt.reshape(batch * seq, D_MODEL)
    xs = x_sample.reshape(m_s, D_MODEL)
    mem = mem_prompt.reshape(batch * N_MEM, D_MODEL)
    cache_k = cache_sb_k.reshape(-1, SB_HEAD_DIM)
    cache_v = cache_sb_v.reshape(-1, SB_HEAD_DIM)
    vec = lambda a: a.reshape(1, -1)
    mem_outs = [[] for _ in range(2)]
    prev_p, prev_s = (), ()
    for l in range(depth):
        w_in_l = w_in[l].astype(BF16)
        w_branch_l = w_branch[l].astype(BF16)
        w_out_l = w_out[l].astype(BF16)
        w_ffn_in_l = w_ffn_in[l].astype(BF16)
        w_ffn_out_l = w_ffn_out[l].astype(BF16)
        proj_args = (vec(g_pre_mix[l]), w_in_l, vec(b_gate[l]), vec(ln_sg_g[l]), vec(ln_sg_b[l]))
        ffn_args = (vec(g_pre_ffn[l]), w_ffn_in_l, w_ffn_out_l, vec(g_post_ffn[l]))

        kv32, kv16 = _mem_kv(mem, vec(g_mem[l]), w_mem_kv[l].astype(BF16))
        proj, *prev_p = _in_proj(xp, *proj_args, prev_p, layer=l, depth=depth, tm=tm_proj, emit_vn=False)
        o_sb = _sb_prompt(proj, batch=batch, seq=seq)
        xp = _mixer(xp, o_sb, proj, w_spatial[l], b_spatial[l].T,
                    kv16[0].reshape(batch, N_MEM, D_MODEL), kv16[1].reshape(batch, N_MEM, D_MODEL),
                    w_branch_l, w_out_l, vec(g_post_mix[l]),
                    tm=tm, chunk=SG_CHUNK, seg=tm, rows_per_mem=seq)
        xp = _ffn(xp, *ffn_args, tm=tm)
        mem_outs[0].append(kv32[0].reshape(batch, N_MEM, N_MEM_HEADS, MEM_HEAD_DIM))
        mem_outs[1].append(kv32[1].reshape(batch, N_MEM, N_MEM_HEADS, MEM_HEAD_DIM))

        proj, *prev_s = _in_proj(xs, *proj_args, prev_s, layer=l, depth=depth, tm=tm_proj, emit_vn=True)
        o_sb = _sb_sample(proj, cache_k, cache_v, layer=l, batch=dec_batch, seq=dec_seq, past=past)
        xs = _mixer(xs, o_sb, proj, w_spatial[l][:, :dec_seq, :dec_seq], b_spatial[l][:, :dec_seq].T,
                    cache_mem_k[l].reshape(dec_batch, N_MEM, D_MODEL).astype(BF16),
                    cache_mem_v[l].reshape(dec_batch, N_MEM, D_MODEL).astype(BF16),
                    w_branch_l, w_out_l, vec(g_post_mix[l]),
                    tm=tm, chunk=dec_seq, seg=dec_seq, rows_per_mem=dec_seq)
        xs = _ffn(xs, *ffn_args, tm=tm)

    heads = (N_SB_HEADS, SB_HEAD_DIM)
    return (xp.reshape(batch, seq, D_MODEL), xs.reshape(dec_batch, dec_seq, D_MODEL),
            prev_p[0].reshape(depth, batch, seq, *heads), prev_p[1].reshape(depth, batch, seq, *heads),
            jnp.stack(mem_outs[0]), jnp.stack(mem_outs[1]),
            prev_s[0].reshape(depth, dec_batch, dec_seq, *heads),
            prev_s[1].reshape(depth, dec_batch, dec_seq, *heads),
            prev_s[2].reshape(depth, dec_batch, dec_seq, D_MODEL))
```

```python
import functools

import jax
import jax.numpy as jnp
from jax import lax
from jax.experimental import pallas as pl
from jax.experimental.pallas import tpu as pltpu

F32 = jnp.float32
BF16 = jnp.bfloat16

D_MODEL = 1024
CHUNK_COLS = 1024
N_SB_HEADS = 8
SB_HEAD_DIM = 128
N_SG_GROUPS = 4
SG_GROUP_DIM = 256
SG_CHUNK = 128
N_MEM = 256
N_MEM_HEADS = 4
MEM_HEAD_DIM = 256
N_BRANCH = 3
D_FF = 2816
EPS = 1e-6

COL_Q, COL_K, COL_V, COL_U, COL_VSG, COL_QMEM, COL_GATE = 0, 1, 2, 3, 4, 5, 6
N_COL_BLOCKS = COL_GATE + N_BRANCH
IN_PROJ_ORDER = (COL_U, COL_Q, COL_VSG, COL_K, COL_GATE, COL_V, COL_GATE + 1, COL_GATE + 2, COL_QMEM)

SB_TILE = 128
SB_WINDOW = 3
SB_CHAINS = 8
SB_LOG2_WEIGHT_FLOOR = -150.04
ROW_PARTS = 2
FF_CHUNK = 256
LOG2_E = 1.4426950408889634

VMEM_LIMIT = 56 * 1024 * 1024


def _dot(a, b):
    return jnp.dot(a, b, preferred_element_type=F32)


def _dot_nt(a, b):
    return lax.dot_general(a, b, (((1,), (1,)), ((), ())), preferred_element_type=F32)


def _rmsnorm(x, g):
    return x * lax.rsqrt(jnp.mean(x * x, axis=-1, keepdims=True) + EPS) * g


def _gelu(x):
    return 0.5 * x * (1.0 + jnp.tanh(0.7978845608028654 * (x + 0.044715 * (x * x * x))))


def _sigmoid(x):
    return 1.0 / (1.0 + jnp.exp(-x))


def _layer_weight(layer, *shape):
    zeros = (0,) * len(shape)
    return pl.BlockSpec((None, *shape), lambda *_: (layer, *zeros), pipeline_mode=pl.Buffered(1))


def _params(*semantics):
    return pltpu.CompilerParams(dimension_semantics=semantics, vmem_limit_bytes=VMEM_LIMIT)


def _store_heads(dst_ref, acc):
    tm = acc.shape[0]
    for r in range(dst_ref.shape[0]):
        for h in range(N_SB_HEADS):
            dst_ref[r, pl.ds(h, tm, stride=N_SB_HEADS), :] = acc[:, h * SB_HEAD_DIM:(h + 1) * SB_HEAD_DIM]


def _in_proj_kernel(*refs, n_prev):
    (x_ref, g_ref, w_ref, bg_ref, lng_ref, lnb_ref), refs = refs[:6], refs[6 + n_prev:]
    proj_ref, k_ref, v_ref, vn_ref = refs
    h = _rmsnorm(x_ref[...], g_ref[...]).astype(BF16)
    for j in IN_PROJ_ORDER:
        cols = slice(j * CHUNK_COLS, (j + 1) * CHUNK_COLS)
        acc = _dot(h, w_ref[:, cols])
        if j == COL_K:
            _store_heads(k_ref, acc)
        elif j == COL_V:
            _store_heads(v_ref, acc)
        elif j == COL_U:
            acc = _gelu(acc)
        elif j == COL_VSG:
            a = _gelu(acc)
            mu = jnp.mean(a, axis=-1, keepdims=True)
            d = a - mu
            var = jnp.mean(d * d, axis=-1, keepdims=True)
            acc = d * lax.rsqrt(var + EPS) * lng_ref[...] + lnb_ref[...]
            if vn_ref is not None:
                for r in range(vn_ref.shape[0]):
                    vn_ref[r] = acc
        elif j >= COL_GATE:
            g = slice((j - COL_GATE) * CHUNK_COLS, (j - COL_GATE + 1) * CHUNK_COLS)
            acc = _sigmoid(acc + bg_ref[:, g])
        proj_ref[:, cols] = acc.astype(BF16)


def _in_proj(x, g_pre, w_in, b_gate, ln_g, ln_b, prev, *, layer, depth, tm, emit_vn):
    m = x.shape[0]
    n_i = m // tm
    d_in = N_COL_BLOCKS * CHUNK_COLS
    row = lambda i: (i, 0)
    resident = lambda shape: pl.BlockSpec(shape, lambda i: (0, 0), pipeline_mode=pl.Buffered(1))
    layers = depth if layer == 0 else 1
    layer_rows = lambda shape: pl.BlockSpec((layers, *shape), lambda i: (layer, i, 0))
    out_shape = [jax.ShapeDtypeStruct((m, d_in), BF16),
                 jax.ShapeDtypeStruct((depth, m * N_SB_HEADS, SB_HEAD_DIM), F32),
                 jax.ShapeDtypeStruct((depth, m * N_SB_HEADS, SB_HEAD_DIM), F32)]
    out_specs = [pl.BlockSpec((tm, d_in), row),
                 layer_rows((tm * N_SB_HEADS, SB_HEAD_DIM)),
                 layer_rows((tm * N_SB_HEADS, SB_HEAD_DIM))]
    if emit_vn:
        out_shape.append(jax.ShapeDtypeStruct((depth, m, CHUNK_COLS), F32))
        out_specs.append(layer_rows((tm, CHUNK_COLS)))
    n_prev = len(prev)
    assert n_prev == (0 if layer == 0 else len(out_shape) - 1)

    def body(*refs):
        if not emit_vn:
            refs = refs + (None,)
        _in_proj_kernel(*refs, n_prev=n_prev)

    return pl.pallas_call(
        body,
        grid=(n_i,),
        in_specs=[pl.BlockSpec((tm, D_MODEL), row),
                  resident((1, D_MODEL)),
                  _layer_weight(layer, D_MODEL, d_in),
                  resident((1, N_BRANCH * CHUNK_COLS)),
                  resident((1, CHUNK_COLS)),
                  resident((1, CHUNK_COLS))]
                 + [pl.BlockSpec(memory_space=pl.ANY)] * n_prev,
        out_specs=out_specs,
        out_shape=out_shape,
        input_output_aliases={6 + p: 1 + p for p in range(n_prev)},
        compiler_params=_params("arbitrary"),
        name="in_proj",
    )(x, g_pre, w_in, b_gate, ln_g, ln_b, *prev)


def _suffix_matrix():
    t = SB_TILE
    j = lax.broadcasted_iota(jnp.int32, (2 * t, 2 * t), 0) % t
    s = lax.broadcasted_iota(jnp.int32, (2 * t, 2 * t), 1)
    return jnp.where((j > s) | (s >= t), 1.0, 0.0).astype(BF16)


def _sb_front(qk, mask):
    zns = [_dot_nt(q, k) * (-(SB_HEAD_DIM ** -0.5) * LOG2_E) for q, k, _ in qk]
    keeps = [jnp.minimum(zn, 0.0) - jnp.log2(1.0 + jnp.exp2(-jnp.abs(zn))) for zn in zns]
    betas = [keep - zn for keep, zn in zip(keeps, zns)]
    pieces = []
    for keep, (_, _, n_tiles) in zip(keeps, qk):
        per_tile = []
        for d in range(n_tiles):
            log_keep = keep[:, d * SB_TILE:(d + 1) * SB_TILE]
            if mask is not None and d == n_tiles - 1:
                log_keep = jnp.where(mask, log_keep, 0.0)
            hi = log_keep.astype(BF16)
            lo = (log_keep - hi.astype(F32)).astype(BF16)
            per_tile.append(jnp.concatenate([hi, lo], axis=1))
        pieces.append(per_tile)
    return betas, pieces


def _sb_back(betas, pieces, vs, carries, mask, u):
    sums = [[_dot(p, u) for p in per_tile] for per_tile in pieces]
    ws = []
    for beta, per_tile, carry in zip(betas, sums, carries):
        n_tiles = len(per_tile)
        w_tiles = [None] * n_tiles
        for d in reversed(range(n_tiles)):
            suffix, total = per_tile[d][:, :SB_TILE], per_tile[d][:, SB_TILE:]
            between = suffix if carry is None else carry + suffix
            w = jnp.exp2(beta[:, d * SB_TILE:(d + 1) * SB_TILE] + between)
            if mask is not None and d == n_tiles - 1:
                w = jnp.where(mask, w, 0.0)
            w_tiles[d] = w.astype(BF16)
            carry = total if carry is None else carry + total
        ws.append((jnp.concatenate(w_tiles, axis=1), carry))
    return [(_dot(w, v), carry) for (w, carry), v in zip(ws, vs)]


def _sb_chains(chains, mask, u):
    betas, pieces = _sb_front([(q, k, n) for q, k, _, n, _ in chains], mask)
    return _sb_back(betas, pieces, [c[2] for c in chains], [c[4] for c in chains], mask, u)


def _live(carry):
    return jnp.max(carry) >= SB_LOG2_WEIGHT_FLOOR


def _sb_prompt_kernel(q_ref, k_ref, v_ref, o_ref, acc_ref, carry_ref):
    t = SB_TILE
    u = _suffix_matrix()
    causal = (lax.broadcasted_iota(jnp.int32, (t, t), 1) < lax.broadcasted_iota(jnp.int32, (t, t), 0))

    def rows(tile, n=1):
        start = tile * t
        return pl.ds(start if isinstance(start, int) else pl.multiple_of(start, t), n * t)

    def keys(qt, n_tiles):
        return rows(qt - (n_tiles - 1), n_tiles)

    def settle(qt, sizes, results):
        def pending(j, carries):
            worst = None
            for s, n in enumerate(sizes):
                c = jnp.where(qt + s - n - j >= 0, carries[s], -jnp.inf)
                worst = c if worst is None else jnp.maximum(worst, c)
            return _live(worst)

        for s, (pv, carry) in enumerate(results):
            acc_ref[s] = pv
            carry_ref[s] = carry
            o_ref[rows(qt + s), :] = pv.astype(BF16)

        def body(state):
            j, _ = state
            kbs = [qt + s - n - j for s, n in enumerate(sizes)]
            tiles = [rows(jnp.maximum(kb, 0)) for kb in kbs]
            stepped = _sb_chains([(q_ref[rows(qt + s), :], k_ref[tiles[s], :], v_ref[tiles[s], :], 1,
                                   carry_ref[s]) for s in range(len(sizes))], None, u)
            carries = []
            for s, (pv, carry) in enumerate(stepped):
                acc = acc_ref[s] + jnp.where(kbs[s] >= 0, pv, 0.0)
                carry = jnp.where(kbs[s] >= 0, carry, carry_ref[s])
                acc_ref[s] = acc
                carry_ref[s] = carry
                o_ref[rows(qt + s), :] = acc.astype(BF16)
                carries.append(carry)
            return j + 1, pending(j + 1, carries)

        lax.while_loop(lambda state: state[1], body, (0, pending(0, [c for _, c in results])))

    def tile_group(qt, sizes):
        settle(qt, sizes, _sb_chains(
            [(q_ref[rows(qt + s), :], k_ref[keys(qt + s, n), :], v_ref[keys(qt + s, n), :], n, None)
             for s, n in enumerate(sizes)], causal, u))

    n_q = q_ref.shape[0] // t
    first = -(-(SB_WINDOW - 1) // SB_CHAINS) * SB_CHAINS
    for qt in range(0, first, SB_CHAINS):
        tile_group(qt, [min(qt + s + 1, SB_WINDOW) for s in range(SB_CHAINS)])

    def step(i, _):
        tile_group(first + SB_CHAINS * i, [SB_WINDOW] * SB_CHAINS)
        return 0

    lax.fori_loop(0, (n_q - first) // SB_CHAINS, step, 0)


def _sb_prompt(proj, *, batch, seq):
    assert seq % (SB_CHAINS * SB_TILE) == 0
    head = lambda c: pl.BlockSpec((seq, SB_HEAD_DIM), lambda b, h: (b, c * N_SB_HEADS + h))
    return pl.pallas_call(
        _sb_prompt_kernel,
        grid=(batch, N_SB_HEADS),
        in_specs=[head(COL_Q), head(COL_K), head(COL_V)],
        out_specs=pl.BlockSpec((seq, SB_HEAD_DIM), lambda b, h: (b, h)),
        out_shape=jax.ShapeDtypeStruct((batch * seq, N_SB_HEADS * SB_HEAD_DIM), BF16),
        scratch_shapes=[pltpu.VMEM((SB_CHAINS, SB_TILE, SB_HEAD_DIM), F32),
                        pltpu.VMEM((SB_CHAINS, SB_TILE, SB_TILE), F32)],
        compiler_params=_params("arbitrary", "arbitrary"),
        name="sb_prompt",
    )(proj, proj, proj)


def _sb_sample_kernel(q_ref, k_ref, v_ref, ck_hbm, cv_hbm, o_ref, kbuf, vbuf, sem, acc_ref, carry_ref,
                      *, cache_row0, past):
    t = SB_TILE
    n_new = q_ref.shape[0]
    n_blocks = past // t
    block_rows = t * N_SB_HEADS
    u = _suffix_matrix()
    base = cache_row0 + pl.program_id(0) * past * N_SB_HEADS

    def copies(kb, slot):
        src = pl.ds(base + kb * block_rows, block_rows)
        return (pltpu.make_async_copy(ck_hbm.at[src, :], kbuf.at[slot], sem.at[0, slot]),
                pltpu.make_async_copy(cv_hbm.at[src, :], vbuf.at[slot], sem.at[1, slot]))

    def start(kb, slot):
        for c in copies(kb, slot):
            c.start()

    def wait(kb, slot):
        for c in copies(kb, slot):
            c.wait()

    start(n_blocks - 1, 0)

    causal = (lax.broadcasted_iota(jnp.int32, (n_new, t), 1) < lax.broadcasted_iota(jnp.int32, (n_new, t), 0))
    pad = jnp.zeros((t - n_new, SB_HEAD_DIM), BF16)
    head_cols = [slice(h * SB_HEAD_DIM, (h + 1) * SB_HEAD_DIM) for h in range(N_SB_HEADS)]
    results = _sb_chains([(q_ref[:, cols], jnp.concatenate([k_ref[:, cols], pad], axis=0),
                           jnp.concatenate([v_ref[:, cols], pad], axis=0), 1, None)
                          for cols in head_cols], causal, u)
    live = False
    for h, (pv, carry) in enumerate(results):
        acc_ref[h] = pv
        carry_ref[h] = carry
        live = live | _live(carry)

    def body(state):
        kb, _ = state
        slot = (n_blocks - 1 - kb) % 2
        wait(kb, slot)

        @pl.when(kb > 0)
        def _():
            start(kb - 1, 1 - slot)

        head_rows = [pl.ds(h, t, stride=N_SB_HEADS) for h in range(N_SB_HEADS)]
        results = _sb_chains([(q_ref[:, head_cols[h]], kbuf[slot, head_rows[h], :].astype(BF16),
                               vbuf[slot, head_rows[h], :].astype(BF16), 1, carry_ref[h])
                              for h in range(N_SB_HEADS)], None, u)
        live = False
        for h, (pv, carry) in enumerate(results):
            acc_ref[h] += pv
            carry_ref[h] = carry
            live = live | _live(carry)
        return kb - 1, live

    kb_end, _ = lax.while_loop(lambda s: (s[0] >= 0) & s[1], body, (n_blocks - 1, live))

    @pl.when(kb_end >= 0)
    def _():
        wait(kb_end, (n_blocks - 1 - kb_end) % 2)

    for h in range(N_SB_HEADS):
        o_ref[:, h * SB_HEAD_DIM:(h + 1) * SB_HEAD_DIM] = acc_ref[h].astype(BF16)


def _sb_sample(proj, cache_k, cache_v, *, layer, batch, seq, past):
    assert past % SB_TILE == 0 and seq <= SB_TILE
    blk = lambda c: pl.BlockSpec((seq, CHUNK_COLS), lambda b: (b, c))
    block_rows = SB_TILE * N_SB_HEADS
    return pl.pallas_call(
        functools.partial(_sb_sample_kernel, cache_row0=layer * batch * past * N_SB_HEADS, past=past),
        grid=(batch,),
        in_specs=[blk(COL_Q), blk(COL_K), blk(COL_V),
                  pl.BlockSpec(memory_space=pl.ANY), pl.BlockSpec(memory_space=pl.ANY)],
        out_specs=pl.BlockSpec((seq, N_SB_HEADS * SB_HEAD_DIM), lambda b: (b, 0)),
        out_shape=jax.ShapeDtypeStruct((batch * seq, N_SB_HEADS * SB_HEAD_DIM), BF16),
        scratch_shapes=[pltpu.VMEM((2, block_rows, SB_HEAD_DIM), F32),
                        pltpu.VMEM((2, block_rows, SB_HEAD_DIM), F32),
                        pltpu.SemaphoreType.DMA((2, 2)),
                        pltpu.VMEM((N_SB_HEADS, seq, SB_HEAD_DIM), F32),
                        pltpu.VMEM((N_SB_HEADS, seq, SB_TILE), F32)],
        compiler_params=_params("arbitrary"),
        name="sb_sample",
    )(proj, proj, proj, cache_k, cache_v)


def _mixer_kernel(x_ref, osb_ref, u_ref, vn_ref, qm_ref, g0_ref, g1_ref, g2_ref, ws_ref, bs_ref,
                  mk_ref, mv_ref, wb_ref, wo_ref, gp_ref, y_ref, osg_ref, omem_ref, *, chunk, seg):
    tm = x_ref.shape[0]
    r = lax.broadcasted_iota(jnp.int32, (chunk, chunk), 0)
    c = lax.broadcasted_iota(jnp.int32, (chunk, chunk), 1)
    for g in range(N_SG_GROUPS):
        w_g = jnp.where(c <= r, ws_ref[g], 0.0).astype(BF16)
        b_g = bs_ref[:, g:g + 1]
        cols = slice(g * SG_GROUP_DIM, (g + 1) * SG_GROUP_DIM)
        for n in range(tm // chunk):
            rows = slice(n * chunk, (n + 1) * chunk)
            sg = _dot(w_g, vn_ref[rows, cols]) + b_g
            osg_ref[rows, cols] = (u_ref[rows, cols].astype(F32) * sg).astype(BF16)
    for s in range(tm // seg):
        rows = slice(s * seg, (s + 1) * seg)
        for h in range(N_MEM_HEADS):
            cols = slice(h * MEM_HEAD_DIM, (h + 1) * MEM_HEAD_DIM)
            sc = _dot_nt(qm_ref[rows, cols], mk_ref[s, :, cols]) * (MEM_HEAD_DIM ** -0.5)
            p = jnp.exp(sc - jnp.max(sc, axis=-1, keepdims=True))
            o = _dot(p.astype(BF16), mv_ref[s, :, cols]) / jnp.sum(p, axis=-1, keepdims=True)
            omem_ref[rows, cols] = o.astype(BF16)
    merged = (g0_ref[...].astype(F32) * _dot(osb_ref[...], wb_ref[0])
              + g1_ref[...].astype(F32) * _dot(osg_ref[...], wb_ref[1])
              + g2_ref[...].astype(F32) * _dot(omem_ref[...], wb_ref[2]))
    out = _dot(merged.astype(BF16), wo_ref[...])
    y_ref[...] = x_ref[...] + _rmsnorm(out, gp_ref[...])


def _mixer(x, o_sb, proj, w_s, b_s_t, mem_k, mem_v, w_branch, w_out, g_post, *, layer, tm, chunk, seg,
           rows_per_mem, mem_v_first=0):
    m = x.shape[0]
    row = lambda i: (i, 0)
    col = lambda cb: pl.BlockSpec((tm, CHUNK_COLS), lambda i: (i, cb))
    mem_blocks = tm // seg
    mem = lambda first: pl.BlockSpec((mem_blocks, N_MEM, D_MODEL),
                                     lambda i: (first + i * tm // rows_per_mem // mem_blocks, 0, 0))
    const2 = lambda i: (0, 0)
    const3 = lambda i: (0, 0, 0)
    return pl.pallas_call(
        functools.partial(_mixer_kernel, chunk=chunk, seg=seg),
        grid=(m // tm,),
        in_specs=[pl.BlockSpec((tm, D_MODEL), row), pl.BlockSpec((tm, D_MODEL), row),
                  col(COL_U), col(COL_VSG), col(COL_QMEM), col(COL_GATE), col(COL_GATE + 1),
                  col(COL_GATE + 2),
                  pl.BlockSpec((N_SG_GROUPS, chunk, chunk), const3),
                  pl.BlockSpec((chunk, N_SG_GROUPS), const2),
                  mem(0), mem(mem_v_first),
                  _layer_weight(layer, N_BRANCH, D_MODEL, D_MODEL),
                  _layer_weight(layer, D_MODEL, D_MODEL),
                  pl.BlockSpec((1, D_MODEL), const2)],
        out_specs=pl.BlockSpec((tm, D_MODEL), row),
        out_shape=jax.ShapeDtypeStruct((m, D_MODEL), F32),
        scratch_shapes=[pltpu.VMEM((tm, D_MODEL), BF16), pltpu.VMEM((tm, D_MODEL), BF16)],
        compiler_params=_params("arbitrary"),
        name="mixer",
    )(x, o_sb, proj, proj, proj, proj, proj, proj, w_s, b_s_t, mem_k, mem_v, w_branch, w_out, g_post)


def _ffn_kernel(x_ref, gpre_ref, wi_ref, wo_ref, gpost_ref, y_ref, act_ref):
    tm = x_ref.shape[0]
    for r in range(0, tm, tm // ROW_PARTS):
        rows = slice(r, r + tm // ROW_PARTS)
        h = _rmsnorm(x_ref[rows, :], gpre_ref[...]).astype(BF16)
        for c in range(0, D_FF, FF_CHUNK):
            a = _dot(h, wi_ref[:, c:c + FF_CHUNK])
            b = _dot(h, wi_ref[:, D_FF + c:D_FF + c + FF_CHUNK])
            act_ref[rows, c:c + FF_CHUNK] = (a * _sigmoid(a) * b).astype(BF16)
        out = _dot(act_ref[rows, :], wo_ref[...])
        y_ref[rows, :] = x_ref[rows, :] + _rmsnorm(out, gpost_ref[...])


def _ffn(x, g_pre, w_ffn_in, w_ffn_out, g_post, *, layer, tm):
    m = x.shape[0]
    row = lambda i: (i, 0)
    resident = lambda shape: pl.BlockSpec(shape, lambda i: (0, 0), pipeline_mode=pl.Buffered(1))
    return pl.pallas_call(
        _ffn_kernel,
        grid=(m // tm,),
        in_specs=[pl.BlockSpec((tm, D_MODEL), row),
                  resident((1, D_MODEL)),
                  _layer_weight(layer, D_MODEL, 2 * D_FF),
                  _layer_weight(layer, D_FF, D_MODEL),
                  resident((1, D_MODEL))],
        out_specs=pl.BlockSpec((tm, D_MODEL), row),
        out_shape=jax.ShapeDtypeStruct((m, D_MODEL), F32),
        scratch_shapes=[pltpu.VMEM((tm, D_FF), BF16)],
        compiler_params=_params("arbitrary"),
        name="ffn",
    )(x, g_pre, w_ffn_in, w_ffn_out, g_post)


def _mem_kv_kernel(mem_ref, g_ref, w_ref, kv32_ref, kv16_ref):
    kv = _dot(_rmsnorm(mem_ref[...], g_ref[...]).astype(BF16), w_ref[...].astype(BF16))
    kv32_ref[0] = kv
    kv16_ref[0] = kv.astype(BF16)


def _mem_kv(mem, g_mem, w_mem_kv, *, layer):
    m = mem.shape[0]
    out = pl.BlockSpec((1, m, D_MODEL), lambda j: (j, 0, 0))
    return pl.pallas_call(
        _mem_kv_kernel,
        grid=(2,),
        in_specs=[pl.BlockSpec((m, D_MODEL), lambda j: (0, 0)),
                  pl.BlockSpec((1, D_MODEL), lambda j: (0, 0)),
                  pl.BlockSpec((None, D_MODEL, D_MODEL), lambda j: (layer, 0, j))],
        out_specs=[out, out],
        out_shape=[jax.ShapeDtypeStruct((2, m, D_MODEL), F32),
                   jax.ShapeDtypeStruct((2, m, D_MODEL), BF16)],
        compiler_params=_params("arbitrary"),
        name="mem_kv",
    )(mem, g_mem, w_mem_kv)


def kernel(x_prompt, x_sample, cache_sb_k, cache_sb_v, cache_mem_k, cache_mem_v, mem_prompt,
           g_pre_mix, w_in, b_gate, ln_sg_g, ln_sg_b, w_spatial, b_spatial, g_mem, w_mem_kv,
           w_branch, w_out, g_post_mix, g_pre_ffn, w_ffn_in, w_ffn_out, g_post_ffn):
    depth = w_in.shape[0]
    batch, seq, _ = x_prompt.shape
    dec_batch, dec_seq, _ = x_sample.shape
    past = cache_sb_k.shape[2]
    tm = 512
    tm_proj = 256
    m_s = dec_batch * dec_seq
    assert (batch * seq) % tm_proj == 0 and seq % tm == 0 and m_s == tm

    xp = x_prompt.reshape(batch * seq, D_MODEL)
    xs = x_sample.reshape(m_s, D_MODEL)
    mem = mem_prompt.reshape(batch * N_MEM, D_MODEL)
    cache_k = cache_sb_k.reshape(-1, SB_HEAD_DIM)
    cache_v = cache_sb_v.reshape(-1, SB_HEAD_DIM)
    vec = lambda a: a.reshape(1, -1)
    mem_outs = [[] for _ in range(2)]
    prev_p, prev_s = (), ()
    w_in, w_branch, w_out, w_ffn_in, w_ffn_out = (
        w.astype(BF16) for w in (w_in, w_branch, w_out, w_ffn_in, w_ffn_out))
    for l in range(depth):
        proj_args = (vec(g_pre_mix[l]), w_in, vec(b_gate[l]), vec(ln_sg_g[l]), vec(ln_sg_b[l]))
        ffn_args = (vec(g_pre_ffn[l]), w_ffn_in, w_ffn_out, vec(g_post_ffn[l]))

        kv32, kv16 = _mem_kv(mem, vec(g_mem[l]), w_mem_kv, layer=l)
        proj, *prev_p = _in_proj(xp, *proj_args, prev_p, layer=l, depth=depth, tm=tm_proj, emit_vn=False)
        o_sb = _sb_prompt(proj, batch=batch, seq=seq)
        mem_kv16 = kv16.reshape(2 * batch, N_MEM, D_MODEL)
        xp = _mixer(xp, o_sb, proj, w_spatial[l], b_spatial[l].T, mem_kv16, mem_kv16,
                    w_branch, w_out, vec(g_post_mix[l]),
                    layer=l, tm=tm, chunk=SG_CHUNK, seg=tm, rows_per_mem=seq, mem_v_first=batch)
        xp = _ffn(xp, *ffn_args, layer=l, tm=2 * tm)
        mem_outs[0].append(kv32[0].reshape(batch, N_MEM, N_MEM_HEADS, MEM_HEAD_DIM))
        mem_outs[1].append(kv32[1].reshape(batch, N_MEM, N_MEM_HEADS, MEM_HEAD_DIM))

        proj, *prev_s = _in_proj(xs, *proj_args, prev_s, layer=l, depth=depth, tm=tm_proj, emit_vn=True)
        o_sb = _sb_sample(proj, cache_k, cache_v, layer=l, batch=dec_batch, seq=dec_seq, past=past)
        xs = _mixer(xs, o_sb, proj, w_spatial[l][:, :dec_seq, :dec_seq], b_spatial[l][:, :dec_seq].T,
                    cache_mem_k[l].reshape(dec_batch, N_MEM, D_MODEL).astype(BF16),
                    cache_mem_v[l].reshape(dec_batch, N_MEM, D_MODEL).astype(BF16),
                    w_branch, w_out, vec(g_post_mix[l]),
                    layer=l, tm=tm, chunk=dec_seq, seg=dec_seq, rows_per_mem=dec_seq)
        xs = _ffn(xs, *ffn_args, layer=l, tm=tm)

    heads = (N_SB_HEADS, SB_HEAD_DIM)
    return (xp.reshape(batch, seq, D_MODEL), xs.reshape(dec_batch, dec_seq, D_MODEL),
            prev_p[0].reshape(depth, batch, seq, *heads), prev_p[1].reshape(depth, batch, seq, *heads),
            jnp.stack(mem_outs[0]), jnp.stack(mem_outs[1]),
            prev_s[0].reshape(depth, dec_batch, dec_seq, *heads),
            prev_s[1].reshape(depth, dec_batch, dec_seq, *heads),
            prev_s[2].reshape(depth, dec_batch, dec_seq, D_MODEL))
```

```python
import functools

import jax
import jax.numpy as jnp
from jax import lax
from jax.experimental import pallas as pl
from jax.experimental.pallas import tpu as pltpu

F32 = jnp.float32
BF16 = jnp.bfloat16

D_MODEL = 1024
CHUNK_COLS = 1024
N_SB_HEADS = 8
SB_HEAD_DIM = 128
N_SG_GROUPS = 4
SG_GROUP_DIM = 256
SG_CHUNK = 128
N_MEM = 256
N_MEM_HEADS = 4
MEM_HEAD_DIM = 256
N_BRANCH = 3
D_FF = 2816
EPS = 1e-6

COL_Q, COL_K, COL_V, COL_U, COL_VSG, COL_QMEM, COL_GATE = 0, 1, 2, 3, 4, 5, 6
N_COL_BLOCKS = COL_GATE + N_BRANCH
IN_PROJ_ORDER = (COL_U, COL_Q, COL_VSG, COL_K, COL_GATE, COL_V, COL_GATE + 1, COL_GATE + 2, COL_QMEM)

SB_TILE = 128
SB_WINDOW = 3
SB_CACHE_SLOTS = 4
SB_CHAINS = 8
SB_LOG2_WEIGHT_FLOOR = -150.04
ROW_PARTS = 2
FF_CHUNK = 256
LOG2_E = 1.4426950408889634

VMEM_LIMIT = 56 * 1024 * 1024


def _dot(a, b):
    return jnp.dot(a, b, preferred_element_type=F32)


def _dot_nt(a, b):
    return lax.dot_general(a, b, (((1,), (1,)), ((), ())), preferred_element_type=F32)


def _rmsnorm(x, g):
    return x * lax.rsqrt(jnp.mean(x * x, axis=-1, keepdims=True) + EPS) * g


def _gelu(x):
    c, k = 0.7978845608028654, 0.044715
    half = 0.5 * x
    return half + half * jnp.tanh(x * (c + (c * k) * (x * x)))


def _sigmoid(x):
    return 0.5 + 0.5 * jnp.tanh(0.5 * x)


def _layer_weight(layer, *shape):
    zeros = (0,) * len(shape)
    return pl.BlockSpec((None, *shape), lambda *_: (layer, *zeros), pipeline_mode=pl.Buffered(1))


def _params(*semantics):
    return pltpu.CompilerParams(dimension_semantics=semantics, vmem_limit_bytes=VMEM_LIMIT)


def _store_heads(dst_ref, acc):
    tm = acc.shape[0]
    for h in range(N_SB_HEADS):
        head = acc[:, h * SB_HEAD_DIM:(h + 1) * SB_HEAD_DIM]
        dst_ref[0, pl.ds(h, tm, stride=N_SB_HEADS), :] = head
        for r in range(1, dst_ref.shape[0]):
            dst_ref[r, pl.ds(h * tm, tm), :] = head


def _in_proj_kernel(*refs, n_prev):
    (x_ref, g_ref, w_ref, bg_ref, lng_ref, lnb_ref), refs = refs[:6], refs[6 + n_prev:]
    proj_ref, k_ref, v_ref, vn_ref = refs
    h = _rmsnorm(x_ref[...], g_ref[...]).astype(BF16)
    for j in IN_PROJ_ORDER:
        cols = slice(j * CHUNK_COLS, (j + 1) * CHUNK_COLS)
        acc = _dot(h, w_ref[:, cols])
        if j == COL_K:
            _store_heads(k_ref, acc)
        elif j == COL_V:
            _store_heads(v_ref, acc)
        elif j == COL_U:
            acc = _gelu(acc)
        elif j == COL_VSG:
            a = _gelu(acc)
            mu = jnp.mean(a, axis=-1, keepdims=True)
            d = a - mu
            var = jnp.mean(d * d, axis=-1, keepdims=True)
            acc = d * lax.rsqrt(var + EPS) * lng_ref[...] + lnb_ref[...]
            if vn_ref is not None:
                for r in range(vn_ref.shape[0]):
                    vn_ref[r] = acc
        elif j >= COL_GATE:
            g = slice((j - COL_GATE) * CHUNK_COLS, (j - COL_GATE + 1) * CHUNK_COLS)
            acc = _sigmoid(acc + bg_ref[:, g])
        proj_ref[:, cols] = acc.astype(BF16)


def _in_proj(x, g_pre, w_in, b_gate, ln_g, ln_b, prev, *, layer, depth, tm, emit_vn):
    m = x.shape[0]
    n_i = m // tm
    d_in = N_COL_BLOCKS * CHUNK_COLS
    row = lambda i: (i, 0)
    resident = lambda shape: pl.BlockSpec(shape, lambda i: (0, 0), pipeline_mode=pl.Buffered(1))
    layers = depth if layer == 0 else 1
    layer_rows = lambda shape: pl.BlockSpec((layers, *shape), lambda i: (layer, i, 0))
    out_shape = [jax.ShapeDtypeStruct((m, d_in), BF16),
                 jax.ShapeDtypeStruct((depth, m * N_SB_HEADS, SB_HEAD_DIM), F32),
                 jax.ShapeDtypeStruct((depth, m * N_SB_HEADS, SB_HEAD_DIM), F32)]
    out_specs = [pl.BlockSpec((tm, d_in), row),
                 layer_rows((tm * N_SB_HEADS, SB_HEAD_DIM)),
                 layer_rows((tm * N_SB_HEADS, SB_HEAD_DIM))]
    if emit_vn:
        out_shape.append(jax.ShapeDtypeStruct((depth, m, CHUNK_COLS), F32))
        out_specs.append(layer_rows((tm, CHUNK_COLS)))
    n_prev = len(prev)
    assert n_prev == (0 if layer == 0 else len(out_shape) - 1)

    def body(*refs):
        if not emit_vn:
            refs = refs + (None,)
        _in_proj_kernel(*refs, n_prev=n_prev)

    return pl.pallas_call(
        body,
        grid=(n_i,),
        in_specs=[pl.BlockSpec((tm, D_MODEL), row),
                  resident((1, D_MODEL)),
                  _layer_weight(layer, D_MODEL, d_in),
                  resident((1, N_BRANCH * CHUNK_COLS)),
                  resident((1, CHUNK_COLS)),
                  resident((1, CHUNK_COLS))]
                 + [pl.BlockSpec(memory_space=pl.ANY)] * n_prev,
        out_specs=out_specs,
        out_shape=out_shape,
        input_output_aliases={6 + p: 1 + p for p in range(n_prev)},
        compiler_params=_params("arbitrary"),
        name="in_proj",
    )(x, g_pre, w_in, b_gate, ln_g, ln_b, *prev)


def _suffix_matrix():
    t = SB_TILE
    j = lax.broadcasted_iota(jnp.int32, (2 * t, 2 * t), 0) % t
    s = lax.broadcasted_iota(jnp.int32, (2 * t, 2 * t), 1)
    return jnp.where((j > s) | (s >= t), 1.0, 0.0).astype(BF16)


def _sb_front(qk, mask):
    zns = [_dot_nt(q, k) * (-(SB_HEAD_DIM ** -0.5) * LOG2_E) for q, k, _ in qk]
    keeps = [jnp.minimum(zn, 0.0) - jnp.log2(1.0 + jnp.exp2(-jnp.abs(zn))) for zn in zns]
    betas = [keep - zn for keep, zn in zip(keeps, zns)]
    pieces = []
    for keep, (_, _, n_tiles) in zip(keeps, qk):
        per_tile = []
        for d in range(n_tiles):
            log_keep = keep[:, d * SB_TILE:(d + 1) * SB_TILE]
            if mask is not None and d == n_tiles - 1:
                log_keep = jnp.where(mask, log_keep, 0.0)
            hi = log_keep.astype(BF16)
            lo = (log_keep - hi.astype(F32)).astype(BF16)
            per_tile.append(jnp.concatenate([hi, lo], axis=1))
        pieces.append(per_tile)
    return betas, pieces


def _sb_back(betas, pieces, vs, carries, mask, u):
    sums = [[_dot(p, u) for p in per_tile] for per_tile in pieces]
    ws = []
    for beta, per_tile, carry in zip(betas, sums, carries):
        n_tiles = len(per_tile)
        w_tiles = [None] * n_tiles
        for d in reversed(range(n_tiles)):
            suffix, total = per_tile[d][:, :SB_TILE], per_tile[d][:, SB_TILE:]
            between = suffix if carry is None else carry + suffix
            w = jnp.exp2(beta[:, d * SB_TILE:(d + 1) * SB_TILE] + between)
            if mask is not None and d == n_tiles - 1:
                w = jnp.where(mask, w, 0.0)
            w_tiles[d] = w.astype(BF16)
            carry = total if carry is None else carry + total
        ws.append((jnp.concatenate(w_tiles, axis=1), carry))
    return [(_dot(w, v), carry) for (w, carry), v in zip(ws, vs)]


def _sb_chains(chains, mask, u):
    betas, pieces = _sb_front([(q, k, n) for q, k, _, n, _ in chains], mask)
    return _sb_back(betas, pieces, [c[2] for c in chains], [c[4] for c in chains], mask, u)


def _live(carry):
    return jnp.max(carry) >= SB_LOG2_WEIGHT_FLOOR


def _sb_prompt_kernel(q_ref, k_ref, v_ref, o_ref, acc_ref, carry_ref):
    t = SB_TILE
    u = _suffix_matrix()
    causal = (lax.broadcasted_iota(jnp.int32, (t, t), 1) < lax.broadcasted_iota(jnp.int32, (t, t), 0))

    def rows(tile, n=1):
        start = tile * t
        return pl.ds(start if isinstance(start, int) else pl.multiple_of(start, t), n * t)

    def keys(qt, n_tiles):
        return rows(qt - (n_tiles - 1), n_tiles)

    def settle(qt, sizes, results):
        def pending(j, carries):
            worst = None
            for s, n in enumerate(sizes):
                c = jnp.where(qt + s - n - j >= 0, carries[s], -jnp.inf)
                worst = c if worst is None else jnp.maximum(worst, c)
            return _live(worst)

        for s, (pv, carry) in enumerate(results):
            acc_ref[s] = pv
            carry_ref[s] = carry
            o_ref[rows(qt + s), :] = pv.astype(BF16)

        def body(state):
            j, _ = state
            kbs = [qt + s - n - j for s, n in enumerate(sizes)]
            tiles = [rows(jnp.maximum(kb, 0)) for kb in kbs]
            stepped = _sb_chains([(q_ref[rows(qt + s), :], k_ref[tiles[s], :], v_ref[tiles[s], :], 1,
                                   carry_ref[s]) for s in range(len(sizes))], None, u)
            carries = []
            for s, (pv, carry) in enumerate(stepped):
                acc = acc_ref[s] + jnp.where(kbs[s] >= 0, pv, 0.0)
                carry = jnp.where(kbs[s] >= 0, carry, carry_ref[s])
                acc_ref[s] = acc
                carry_ref[s] = carry
                o_ref[rows(qt + s), :] = acc.astype(BF16)
                carries.append(carry)
            return j + 1, pending(j + 1, carries)

        lax.while_loop(lambda state: state[1], body, (0, pending(0, [c for _, c in results])))

    def tile_group(qt, sizes):
        settle(qt, sizes, _sb_chains(
            [(q_ref[rows(qt + s), :], k_ref[keys(qt + s, n), :], v_ref[keys(qt + s, n), :], n, None)
             for s, n in enumerate(sizes)], causal, u))

    n_q = q_ref.shape[0] // t
    first = -(-(SB_WINDOW - 1) // SB_CHAINS) * SB_CHAINS
    for qt in range(0, first, SB_CHAINS):
        tile_group(qt, [min(qt + s + 1, SB_WINDOW) for s in range(SB_CHAINS)])

    def step(i, _):
        tile_group(first + SB_CHAINS * i, [SB_WINDOW] * SB_CHAINS)
        return 0

    lax.fori_loop(0, (n_q - first) // SB_CHAINS, step, 0)


def _sb_prompt(proj, *, batch, seq):
    assert seq % (SB_CHAINS * SB_TILE) == 0
    head = lambda c: pl.BlockSpec((seq, SB_HEAD_DIM), lambda b, h: (b, c * N_SB_HEADS + h))
    return pl.pallas_call(
        _sb_prompt_kernel,
        grid=(batch, N_SB_HEADS),
        in_specs=[head(COL_Q), head(COL_K), head(COL_V)],
        out_specs=pl.BlockSpec((seq, SB_HEAD_DIM), lambda b, h: (b, h)),
        out_shape=jax.ShapeDtypeStruct((batch * seq, N_SB_HEADS * SB_HEAD_DIM), BF16),
        scratch_shapes=[pltpu.VMEM((SB_CHAINS, SB_TILE, SB_HEAD_DIM), F32),
                        pltpu.VMEM((SB_CHAINS, SB_TILE, SB_TILE), F32)],
        compiler_params=_params("arbitrary", "arbitrary"),
        name="sb_prompt",
    )(proj, proj, proj)


def _sb_sample_kernel(q_ref, k_ref, v_ref, ck_hbm, cv_hbm, o_ref, kbuf, vbuf, sem, acc_ref, carry_ref,
                      *, cache_row0, past):
    t = SB_TILE
    n_new = q_ref.shape[0]
    n_blocks = past // t
    block_rows = t * N_SB_HEADS
    u = _suffix_matrix()
    base = cache_row0 + pl.program_id(0) * past * N_SB_HEADS

    def copies(kb, slot):
        src = pl.ds(base + kb * block_rows, block_rows)
        return (pltpu.make_async_copy(ck_hbm.at[src, :], kbuf.at[slot], sem.at[0, slot]),
                pltpu.make_async_copy(cv_hbm.at[src, :], vbuf.at[slot], sem.at[1, slot]))

    def start(kb, slot):
        for c in copies(kb, slot):
            c.start()

    def wait(kb, slot):
        for c in copies(kb, slot):
            c.wait()

    slots = kbuf.shape[0]
    ahead = slots - 1
    for kb in range(n_blocks - 1, max(n_blocks - 1 - ahead, -1), -1):
        start(kb, kb % slots)

    causal = (lax.broadcasted_iota(jnp.int32, (n_new, t), 1) < lax.broadcasted_iota(jnp.int32, (n_new, t), 0))
    pad = jnp.zeros((t - n_new, SB_HEAD_DIM), BF16)
    head_cols = [slice(h * SB_HEAD_DIM, (h + 1) * SB_HEAD_DIM) for h in range(N_SB_HEADS)]
    results = _sb_chains([(q_ref[:, cols], jnp.concatenate([k_ref[:, cols], pad], axis=0),
                           jnp.concatenate([v_ref[:, cols], pad], axis=0), 1, None)
                          for cols in head_cols], causal, u)
    live = False
    for h, (pv, carry) in enumerate(results):
        acc_ref[h] = pv
        carry_ref[h] = carry
        live = live | _live(carry)

    def body(state):
        kb, _ = state
        slot = kb % slots
        wait(kb, slot)

        @pl.when(kb >= ahead)
        def _():
            start(kb - ahead, (kb - ahead) % slots)

        head_rows = [pl.ds(h, t, stride=N_SB_HEADS) for h in range(N_SB_HEADS)]
        results = _sb_chains([(q_ref[:, head_cols[h]], kbuf[slot, head_rows[h], :].astype(BF16),
                               vbuf[slot, head_rows[h], :].astype(BF16), 1, carry_ref[h])
                              for h in range(N_SB_HEADS)], None, u)
        live = False
        for h, (pv, carry) in enumerate(results):
            acc_ref[h] += pv
            carry_ref[h] = carry
            live = live | _live(carry)
        return kb - 1, live

    kb_end, _ = lax.while_loop(lambda s: (s[0] >= 0) & s[1], body, (n_blocks - 1, live))

    for back in range(ahead):
        @pl.when(kb_end - back >= 0)
        def _():
            wait(kb_end - back, (kb_end - back) % slots)

    for h in range(N_SB_HEADS):
        o_ref[:, h * SB_HEAD_DIM:(h + 1) * SB_HEAD_DIM] = acc_ref[h].astype(BF16)


def _sb_sample(proj, cache_k, cache_v, *, layer, batch, seq, past):
    assert past % SB_TILE == 0 and seq <= SB_TILE
    blk = lambda c: pl.BlockSpec((seq, CHUNK_COLS), lambda b: (b, c))
    block_rows = SB_TILE * N_SB_HEADS
    return pl.pallas_call(
        functools.partial(_sb_sample_kernel, cache_row0=layer * batch * past * N_SB_HEADS, past=past),
        grid=(batch,),
        in_specs=[blk(COL_Q), blk(COL_K), blk(COL_V),
                  pl.BlockSpec(memory_space=pl.ANY), pl.BlockSpec(memory_space=pl.ANY)],
        out_specs=pl.BlockSpec((seq, N_SB_HEADS * SB_HEAD_DIM), lambda b: (b, 0)),
        out_shape=jax.ShapeDtypeStruct((batch * seq, N_SB_HEADS * SB_HEAD_DIM), BF16),
        scratch_shapes=[pltpu.VMEM((SB_CACHE_SLOTS, block_rows, SB_HEAD_DIM), F32),
                        pltpu.VMEM((SB_CACHE_SLOTS, block_rows, SB_HEAD_DIM), F32),
                        pltpu.SemaphoreType.DMA((2, SB_CACHE_SLOTS)),
                        pltpu.VMEM((N_SB_HEADS, seq, SB_HEAD_DIM), F32),
                        pltpu.VMEM((N_SB_HEADS, seq, SB_TILE), F32)],
        compiler_params=_params("arbitrary"),
        name="sb_sample",
    )(proj, proj, proj, cache_k, cache_v)


def _mixer_kernel(x_ref, osb_ref, u_ref, vn_ref, qm_ref, g0_ref, g1_ref, g2_ref, ws_ref, bs_ref,
                  mk_ref, mv_ref, wb_ref, wo_ref, gp_ref, y_ref, osg_ref, omem_ref, *, chunk, seg,
                  cache_rows):
    tm = x_ref.shape[0]

    def mem_head(ref, s, h):
        if not cache_rows:
            return ref[s, :, h * MEM_HEAD_DIM:(h + 1) * MEM_HEAD_DIM]
        halves = MEM_HEAD_DIM // SB_HEAD_DIM
        per_token = halves * N_MEM_HEADS
        return jnp.concatenate(
            [ref[pl.ds(s * N_MEM * per_token + c * N_MEM_HEADS + h, N_MEM, stride=per_token), :]
             for c in range(halves)], axis=1).astype(BF16)

    r = lax.broadcasted_iota(jnp.int32, (chunk, chunk), 0)
    c = lax.broadcasted_iota(jnp.int32, (chunk, chunk), 1)
    for g in range(N_SG_GROUPS):
        w_g = jnp.where(c <= r, ws_ref[g], 0.0).astype(BF16)
        b_g = bs_ref[:, g:g + 1]
        cols = slice(g * SG_GROUP_DIM, (g + 1) * SG_GROUP_DIM)
        for n in range(tm // chunk):
            rows = slice(n * chunk, (n + 1) * chunk)
            sg = _dot(w_g, vn_ref[rows, cols]) + b_g
            osg_ref[rows, cols] = (u_ref[rows, cols].astype(F32) * sg).astype(BF16)
    for s in range(tm // seg):
        rows = slice(s * seg, (s + 1) * seg)
        for h in range(N_MEM_HEADS):
            cols = slice(h * MEM_HEAD_DIM, (h + 1) * MEM_HEAD_DIM)
            sc = _dot_nt(qm_ref[rows, cols], mem_head(mk_ref, s, h)) * (MEM_HEAD_DIM ** -0.5)
            p = jnp.exp(sc - jnp.max(sc, axis=-1, keepdims=True))
            o = _dot(p.astype(BF16), mem_head(mv_ref, s, h)) / jnp.sum(p, axis=-1, keepdims=True)
            omem_ref[rows, cols] = o.astype(BF16)
    merged = (g0_ref[...].astype(F32) * _dot(osb_ref[...], wb_ref[0])
              + g1_ref[...].astype(F32) * _dot(osg_ref[...], wb_ref[1])
              + g2_ref[...].astype(F32) * _dot(omem_ref[...], wb_ref[2]))
    out = _dot(merged.astype(BF16), wo_ref[...])
    y_ref[...] = x_ref[...] + _rmsnorm(out, gp_ref[...])


def _mixer(x, o_sb, proj, w_s, b_s_t, mem_k, mem_v, w_branch, w_out, g_post, *, layer, tm, chunk, seg,
           rows_per_mem, mem_v_first=0, cache_rows=False):
    m = x.shape[0]
    row = lambda i: (i, 0)
    col = lambda cb: pl.BlockSpec((tm, CHUNK_COLS), lambda i: (i, cb))
    mem_blocks = tm // seg
    if cache_rows:
        assert m == tm
        layer_rows = mem_k.shape[0] // w_out.shape[0]
        mem = lambda first: pl.BlockSpec((layer_rows, SB_HEAD_DIM), lambda i: (layer, 0),
                                         pipeline_mode=pl.Buffered(1))
    else:
        mem = lambda first: pl.BlockSpec((mem_blocks, N_MEM, D_MODEL),
                                         lambda i: (first + i * tm // rows_per_mem // mem_blocks, 0, 0))
    const2 = lambda i: (0, 0)
    const3 = lambda i: (0, 0, 0)
    return pl.pallas_call(
        functools.partial(_mixer_kernel, chunk=chunk, seg=seg, cache_rows=cache_rows),
        grid=(m // tm,),
        in_specs=[pl.BlockSpec((tm, D_MODEL), row), pl.BlockSpec((tm, D_MODEL), row),
                  col(COL_U), col(COL_VSG), col(COL_QMEM), col(COL_GATE), col(COL_GATE + 1),
                  col(COL_GATE + 2),
                  pl.BlockSpec((N_SG_GROUPS, chunk, chunk), const3),
                  pl.BlockSpec((chunk, N_SG_GROUPS), const2),
                  mem(0), mem(mem_v_first),
                  _layer_weight(layer, N_BRANCH, D_MODEL, D_MODEL),
                  _layer_weight(layer, D_MODEL, D_MODEL),
                  pl.BlockSpec((1, D_MODEL), const2)],
        out_specs=pl.BlockSpec((tm, D_MODEL), row),
        out_shape=jax.ShapeDtypeStruct((m, D_MODEL), F32),
        scratch_shapes=[pltpu.VMEM((tm, D_MODEL), BF16), pltpu.VMEM((tm, D_MODEL), BF16)],
        compiler_params=_params("arbitrary"),
        name="mixer",
    )(x, o_sb, proj, proj, proj, proj, proj, proj, w_s, b_s_t, mem_k, mem_v, w_branch, w_out, g_post)


def _ffn_kernel(x_ref, gpre_ref, wi_ref, wo_ref, gpost_ref, y_ref, act_ref):
    tm = x_ref.shape[0]
    for r in range(0, tm, tm // ROW_PARTS):
        rows = slice(r, r + tm // ROW_PARTS)
        h = _rmsnorm(x_ref[rows, :], gpre_ref[...]).astype(BF16)
        for c in range(0, D_FF, FF_CHUNK):
            a = _dot(h, wi_ref[:, c:c + FF_CHUNK])
            b = _dot(h, wi_ref[:, D_FF + c:D_FF + c + FF_CHUNK])
            act_ref[rows, c:c + FF_CHUNK] = (a * _sigmoid(a) * b).astype(BF16)
        out = _dot(act_ref[rows, :], wo_ref[...])
        y_ref[rows, :] = x_ref[rows, :] + _rmsnorm(out, gpost_ref[...])


def _ffn(x, g_pre, w_ffn_in, w_ffn_out, g_post, *, layer, tm):
    m = x.shape[0]
    row = lambda i: (i, 0)
    resident = lambda shape: pl.BlockSpec(shape, lambda i: (0, 0), pipeline_mode=pl.Buffered(1))
    return pl.pallas_call(
        _ffn_kernel,
        grid=(m // tm,),
        in_specs=[pl.BlockSpec((tm, D_MODEL), row),
                  resident((1, D_MODEL)),
                  _layer_weight(layer, D_MODEL, 2 * D_FF),
                  _layer_weight(layer, D_FF, D_MODEL),
                  resident((1, D_MODEL))],
        out_specs=pl.BlockSpec((tm, D_MODEL), row),
        out_shape=jax.ShapeDtypeStruct((m, D_MODEL), F32),
        scratch_shapes=[pltpu.VMEM((tm, D_FF), BF16)],
        compiler_params=_params("arbitrary"),
        name="ffn",
    )(x, g_pre, w_ffn_in, w_ffn_out, g_post)


def _mem_kv_kernel(mem_ref, g_ref, w_ref, kv32_ref, kv16_ref):
    kv = _dot(_rmsnorm(mem_ref[...], g_ref[...]).astype(BF16), w_ref[...].astype(BF16))
    kv32_ref[0] = kv
    kv16_ref[0] = kv.astype(BF16)


def _mem_kv(mem, g_mem, w_mem_kv, *, layer):
    m = mem.shape[0]
    out = pl.BlockSpec((1, m, D_MODEL), lambda j: (j, 0, 0))
    return pl.pallas_call(
        _mem_kv_kernel,
        grid=(2,),
        in_specs=[pl.BlockSpec((m, D_MODEL), lambda j: (0, 0)),
                  pl.BlockSpec((1, D_MODEL), lambda j: (0, 0)),
                  pl.BlockSpec((None, D_MODEL, D_MODEL), lambda j: (layer, 0, j))],
        out_specs=[out, out],
        out_shape=[jax.ShapeDtypeStruct((2, m, D_MODEL), F32),
                   jax.ShapeDtypeStruct((2, m, D_MODEL), BF16)],
        compiler_params=_params("arbitrary"),
        name="mem_kv",
    )(mem, g_mem, w_mem_kv)


def kernel(x_prompt, x_sample, cache_sb_k, cache_sb_v, cache_mem_k, cache_mem_v, mem_prompt,
           g_pre_mix, w_in, b_gate, ln_sg_g, ln_sg_b, w_spatial, b_spatial, g_mem, w_mem_kv,
           w_branch, w_out, g_post_mix, g_pre_ffn, w_ffn_in, w_ffn_out, g_post_ffn):
    depth = w_in.shape[0]
    batch, seq, _ = x_prompt.shape
    dec_batch, dec_seq, _ = x_sample.shape
    past = cache_sb_k.shape[2]
    tm = 512
    tm_proj = 256
    m_s = dec_batch * dec_seq
    assert (batch * seq) % tm_proj == 0 and seq % tm == 0 and m_s == tm

    xp = x_prompt.reshape(batch * seq, D_MODEL)
    xs = x_sample.reshape(m_s, D_MODEL)
    mem = mem_prompt.reshape(batch * N_MEM, D_MODEL)
    cache_k = cache_sb_k.reshape(-1, SB_HEAD_DIM)
    cache_v = cache_sb_v.reshape(-1, SB_HEAD_DIM)

    def stored_rows(c):
        d, b, n, h, w = c.shape
        halves = w // SB_HEAD_DIM
        return (c.reshape(d, b, n, h, halves, SB_HEAD_DIM).transpose(0, 1, 2, 4, 3, 5)
                .reshape(-1, SB_HEAD_DIM))

    mem_cache_k, mem_cache_v = stored_rows(cache_mem_k), stored_rows(cache_mem_v)
    vec = lambda a: a.reshape(1, -1)
    mem_outs = [[] for _ in range(2)]
    prev_p, prev_s = (), ()
    w_in, w_branch, w_out, w_ffn_in, w_ffn_out = (
        w.astype(BF16) for w in (w_in, w_branch, w_out, w_ffn_in, w_ffn_out))
    for l in range(depth):
        proj_args = (vec(g_pre_mix[l]), w_in, vec(b_gate[l]), vec(ln_sg_g[l]), vec(ln_sg_b[l]))
        ffn_args = (vec(g_pre_ffn[l]), w_ffn_in, w_ffn_out, vec(g_post_ffn[l]))

        kv32, kv16 = _mem_kv(mem, vec(g_mem[l]), w_mem_kv, layer=l)
        proj, *prev_p = _in_proj(xp, *proj_args, prev_p, layer=l, depth=depth, tm=tm_proj, emit_vn=False)
        o_sb = _sb_prompt(proj, batch=batch, seq=seq)
        mem_kv16 = kv16.reshape(2 * batch, N_MEM, D_MODEL)
        xp = _mixer(xp, o_sb, proj, w_spatial[l], b_spatial[l].T, mem_kv16, mem_kv16,
                    w_branch, w_out, vec(g_post_mix[l]),
                    layer=l, tm=tm, chunk=SG_CHUNK, seg=tm, rows_per_mem=seq, mem_v_first=batch)
        xp = _ffn(xp, *ffn_args, layer=l, tm=2 * tm)
        mem_outs[0].append(kv32[0].reshape(batch, N_MEM, N_MEM_HEADS, MEM_HEAD_DIM))
        mem_outs[1].append(kv32[1].reshape(batch, N_MEM, N_MEM_HEADS, MEM_HEAD_DIM))

        proj, *prev_s = _in_proj(xs, *proj_args, prev_s, layer=l, depth=depth, tm=tm_proj, emit_vn=True)
        o_sb = _sb_sample(proj, cache_k, cache_v, layer=l, batch=dec_batch, seq=dec_seq, past=past)
        xs = _mixer(xs, o_sb, proj, w_spatial[l][:, :dec_seq, :dec_seq], b_spatial[l][:, :dec_seq].T,
                    mem_cache_k, mem_cache_v, w_branch, w_out, vec(g_post_mix[l]),
                    layer=l, tm=tm, chunk=dec_seq, seg=dec_seq, rows_per_mem=dec_seq, cache_rows=True)
        xs = _ffn(xs, *ffn_args, layer=l, tm=tm)

    heads = (N_SB_HEADS, SB_HEAD_DIM)
    return (xp.reshape(batch, seq, D_MODEL), xs.reshape(dec_batch, dec_seq, D_MODEL),
            prev_p[0].reshape(depth, batch, seq, *heads), prev_p[1].reshape(depth, batch, seq, *heads),
            jnp.stack(mem_outs[0]), jnp.stack(mem_outs[1]),
            prev_s[0].reshape(depth, dec_batch, dec_seq, *heads),
            prev_s[1].reshape(depth, dec_batch, dec_seq, *heads),
            prev_s[2].reshape(depth, dec_batch, dec_seq, D_MODEL))
```

```python
import functools

import jax
import jax.numpy as jnp
from jax import lax
from jax.experimental import pallas as pl
from jax.experimental.pallas import tpu as pltpu

F32 = jnp.float32
BF16 = jnp.bfloat16

D_MODEL = 1024
CHUNK_COLS = 1024
N_SB_HEADS = 8
SB_HEAD_DIM = 128
N_SG_GROUPS = 4
SG_GROUP_DIM = 256
SG_CHUNK = 128
N_MEM = 256
N_MEM_HEADS = 4
MEM_HEAD_DIM = 256
N_BRANCH = 3
D_FF = 2816
EPS = 1e-6

COL_Q, COL_K, COL_V, COL_U, COL_VSG, COL_QMEM, COL_GATE = 0, 1, 2, 3, 4, 5, 6
N_COL_BLOCKS = COL_GATE + N_BRANCH
IN_PROJ_ORDER = (COL_U, COL_Q, COL_VSG, COL_K, COL_GATE, COL_V, COL_GATE + 1, COL_GATE + 2, COL_QMEM)

SB_TILE = 128
SB_WINDOW = 3
SB_CACHE_SLOTS = 4
SB_CHAINS = 8
SB_LOG2_WEIGHT_FLOOR = -150.04
ROW_PARTS = 2
FF_CHUNK = 256
LOG2_E = 1.4426950408889634

VMEM_LIMIT = 56 * 1024 * 1024


def _dot(a, b):
    return jnp.dot(a, b, preferred_element_type=F32)


def _dot_nt(a, b):
    return lax.dot_general(a, b, (((1,), (1,)), ((), ())), preferred_element_type=F32)


def _rmsnorm(x, g):
    return x * lax.rsqrt(jnp.mean(x * x, axis=-1, keepdims=True) + EPS) * g


def _gelu(x):
    c, k = 0.7978845608028654, 0.044715
    half = 0.5 * x
    return half + half * jnp.tanh(x * (c + (c * k) * (x * x)))


def _sigmoid(x):
    return 0.5 + 0.5 * jnp.tanh(0.5 * x)


def _params(*semantics):
    return pltpu.CompilerParams(dimension_semantics=semantics, vmem_limit_bytes=VMEM_LIMIT)


def _store_heads(dst_ref, acc):
    tm = acc.shape[0]
    for h in range(N_SB_HEADS):
        head = acc[:, h * SB_HEAD_DIM:(h + 1) * SB_HEAD_DIM]
        dst_ref[0, pl.ds(h, tm, stride=N_SB_HEADS), :] = head
        for r in range(1, dst_ref.shape[0]):
            dst_ref[r, pl.ds(h * tm, tm), :] = head


def _in_proj_kernel(*refs, n_prev, layer):
    (x_ref, g_ref, w_hbm, bg_ref, lng_ref, lnb_ref), refs = refs[:6], refs[6 + n_prev:]
    proj_ref, k_ref, v_ref, vn_ref, w_ref, stage_ref, sem = refs

    def fetch(n, slot):
        cols = pl.ds(IN_PROJ_ORDER[n] * CHUNK_COLS, CHUNK_COLS)
        return pltpu.make_async_copy(w_hbm.at[layer, :, cols], stage_ref.at[slot], sem.at[slot])

    def run(first_step):
        h = _rmsnorm(x_ref[...], g_ref[...]).astype(BF16)
        if first_step:
            fetch(0, 0).start()
            fetch(1, 1).start()
        for n, j in enumerate(IN_PROJ_ORDER):
            cols = slice(j * CHUNK_COLS, (j + 1) * CHUNK_COLS)
            if first_step:
                slot = n % 2
                fetch(n, slot).wait()
                w_ref[:, cols] = stage_ref[slot].astype(BF16)
                if n + 2 < N_COL_BLOCKS:
                    fetch(n + 2, slot).start()
            acc = _dot(h, w_ref[:, cols])
            if j == COL_K:
                _store_heads(k_ref, acc)
            elif j == COL_V:
                _store_heads(v_ref, acc)
            elif j == COL_U:
                acc = _gelu(acc)
            elif j == COL_VSG:
                a = _gelu(acc)
                mu = jnp.mean(a, axis=-1, keepdims=True)
                d = a - mu
                var = jnp.mean(d * d, axis=-1, keepdims=True)
                acc = d * lax.rsqrt(var + EPS) * lng_ref[...] + lnb_ref[...]
                if vn_ref is not None:
                    for r in range(vn_ref.shape[0]):
                        vn_ref[r] = acc
            elif j >= COL_GATE:
                g = slice((j - COL_GATE) * CHUNK_COLS, (j - COL_GATE + 1) * CHUNK_COLS)
                acc = _sigmoid(acc + bg_ref[:, g])
            proj_ref[:, cols] = acc.astype(BF16)

    pl.when(pl.program_id(0) == 0)(lambda: run(True))
    pl.when(pl.program_id(0) > 0)(lambda: run(False))


def _in_proj(x, g_pre, w_in, b_gate, ln_g, ln_b, prev, *, layer, depth, tm, emit_vn):
    m = x.shape[0]
    n_i = m // tm
    d_in = N_COL_BLOCKS * CHUNK_COLS
    row = lambda i: (i, 0)
    resident = lambda shape: pl.BlockSpec(shape, lambda i: (0, 0), pipeline_mode=pl.Buffered(1))
    layers = depth if layer == 0 else 1
    layer_rows = lambda shape: pl.BlockSpec((layers, *shape), lambda i: (layer, i, 0))
    out_shape = [jax.ShapeDtypeStruct((m, d_in), BF16),
                 jax.ShapeDtypeStruct((depth, m * N_SB_HEADS, SB_HEAD_DIM), F32),
                 jax.ShapeDtypeStruct((depth, m * N_SB_HEADS, SB_HEAD_DIM), F32)]
    out_specs = [pl.BlockSpec((tm, d_in), row),
                 layer_rows((tm * N_SB_HEADS, SB_HEAD_DIM)),
                 layer_rows((tm * N_SB_HEADS, SB_HEAD_DIM))]
    if emit_vn:
        out_shape.append(jax.ShapeDtypeStruct((depth, m, CHUNK_COLS), F32))
        out_specs.append(layer_rows((tm, CHUNK_COLS)))
    n_prev = len(prev)
    assert n_prev == (0 if layer == 0 else len(out_shape) - 1)

    n_out = len(out_shape)

    def body(*refs):
        if not emit_vn:
            refs = refs[:6 + n_prev + n_out] + (None,) + refs[6 + n_prev + n_out:]
        _in_proj_kernel(*refs, n_prev=n_prev, layer=layer)

    return pl.pallas_call(
        body,
        grid=(n_i,),
        in_specs=[pl.BlockSpec((tm, D_MODEL), row),
                  resident((1, D_MODEL)),
                  pl.BlockSpec(memory_space=pl.ANY),
                  resident((1, N_BRANCH * CHUNK_COLS)),
                  resident((1, CHUNK_COLS)),
                  resident((1, CHUNK_COLS))]
                 + [pl.BlockSpec(memory_space=pl.ANY)] * n_prev,
        out_specs=out_specs,
        out_shape=out_shape,
        input_output_aliases={6 + p: 1 + p for p in range(n_prev)},
        scratch_shapes=[pltpu.VMEM((D_MODEL, d_in), BF16),
                        pltpu.VMEM((2, D_MODEL, CHUNK_COLS), F32),
                        pltpu.SemaphoreType.DMA((2,))],
        compiler_params=_params("arbitrary"),
        name="in_proj",
    )(x, g_pre, w_in, b_gate, ln_g, ln_b, *prev)


def _suffix_matrix():
    t = SB_TILE
    j = lax.broadcasted_iota(jnp.int32, (2 * t, 2 * t), 0) % t
    s = lax.broadcasted_iota(jnp.int32, (2 * t, 2 * t), 1)
    return jnp.where((j > s) | (s >= t), 1.0, 0.0).astype(BF16)


def _sb_scores(q, k):
    return _dot_nt(q, k) * (-(SB_HEAD_DIM ** -0.5) * LOG2_E)


def _sb_front(zns, tiles, mask):
    keeps = [jnp.minimum(zn, 0.0) - jnp.log2(1.0 + jnp.exp2(-jnp.abs(zn))) for zn in zns]
    betas = [keep - zn for keep, zn in zip(keeps, zns)]
    pieces = []
    for keep, n_tiles in zip(keeps, tiles):
        per_tile = []
        for d in range(n_tiles):
            log_keep = keep[:, d * SB_TILE:(d + 1) * SB_TILE]
            if mask is not None and d == n_tiles - 1:
                log_keep = jnp.where(mask, log_keep, 0.0)
            hi = log_keep.astype(BF16)
            lo = (log_keep - hi.astype(F32)).astype(BF16)
            per_tile.append(jnp.concatenate([hi, lo], axis=1))
        pieces.append(per_tile)
    return betas, pieces


def _sb_back(betas, pieces, vs, carries, mask, u):
    sums = [[_dot(p, u) for p in per_tile] for per_tile in pieces]
    ws = []
    for beta, per_tile, carry in zip(betas, sums, carries):
        n_tiles = len(per_tile)
        w_tiles = [None] * n_tiles
        for d in reversed(range(n_tiles)):
            suffix, total = per_tile[d][:, :SB_TILE], per_tile[d][:, SB_TILE:]
            between = suffix if carry is None else carry + suffix
            w = jnp.exp2(beta[:, d * SB_TILE:(d + 1) * SB_TILE] + between)
            if mask is not None and d == n_tiles - 1:
                w = jnp.where(mask, w, 0.0)
            w_tiles[d] = w.astype(BF16)
            carry = total if carry is None else carry + total
        ws.append((jnp.concatenate(w_tiles, axis=1), carry))
    return [(_dot(w, v), carry) for (w, carry), v in zip(ws, vs)]


def _sb_chains(chains, mask, u):
    betas, pieces = _sb_front([_sb_scores(q, k) for q, k, _, _, _ in chains], [c[3] for c in chains], mask)
    return _sb_back(betas, pieces, [c[2] for c in chains], [c[4] for c in chains], mask, u)


def _live(carry):
    return jnp.max(carry) >= SB_LOG2_WEIGHT_FLOOR


def _sb_prompt_kernel(q_ref, k_ref, v_ref, o_ref, acc_ref, carry_ref):
    t = SB_TILE
    u = _suffix_matrix()
    causal = (lax.broadcasted_iota(jnp.int32, (t, t), 1) < lax.broadcasted_iota(jnp.int32, (t, t), 0))

    def rows(tile, n=1):
        start = tile * t
        return pl.ds(start if isinstance(start, int) else pl.multiple_of(start, t), n * t)

    def keys(qt, n_tiles):
        return rows(qt - (n_tiles - 1), n_tiles)

    def settle(qt, sizes, results):
        def pending(j, carries):
            worst = None
            for s, n in enumerate(sizes):
                c = jnp.where(qt + s - n - j >= 0, carries[s], -jnp.inf)
                worst = c if worst is None else jnp.maximum(worst, c)
            return _live(worst)

        for s, (pv, carry) in enumerate(results):
            acc_ref[s] = pv
            carry_ref[s] = carry
            o_ref[rows(qt + s), :] = pv.astype(BF16)

        def body(state):
            j, _ = state
            kbs = [qt + s - n - j for s, n in enumerate(sizes)]
            tiles = [rows(jnp.maximum(kb, 0)) for kb in kbs]
            stepped = _sb_chains([(q_ref[rows(qt + s), :], k_ref[tiles[s], :], v_ref[tiles[s], :], 1,
                                   carry_ref[s]) for s in range(len(sizes))], None, u)
            carries = []
            for s, (pv, carry) in enumerate(stepped):
                acc = acc_ref[s] + jnp.where(kbs[s] >= 0, pv, 0.0)
                carry = jnp.where(kbs[s] >= 0, carry, carry_ref[s])
                acc_ref[s] = acc
                carry_ref[s] = carry
                o_ref[rows(qt + s), :] = acc.astype(BF16)
                carries.append(carry)
            return j + 1, pending(j + 1, carries)

        lax.while_loop(lambda state: state[1], body, (0, pending(0, [c for _, c in results])))

    def tile_group(qt, sizes):
        settle(qt, sizes, _sb_chains(
            [(q_ref[rows(qt + s), :], k_ref[keys(qt + s, n), :], v_ref[keys(qt + s, n), :], n, None)
             for s, n in enumerate(sizes)], causal, u))

    n_q = q_ref.shape[0] // t
    first = -(-(SB_WINDOW - 1) // SB_CHAINS) * SB_CHAINS
    for qt in range(0, first, SB_CHAINS):
        tile_group(qt, [min(qt + s + 1, SB_WINDOW) for s in range(SB_CHAINS)])

    def step(i, _):
        tile_group(first + SB_CHAINS * i, [SB_WINDOW] * SB_CHAINS)
        return 0

    lax.fori_loop(0, (n_q - first) // SB_CHAINS, step, 0)


def _sb_prompt(proj, *, batch, seq):
    assert seq % (SB_CHAINS * SB_TILE) == 0
    head = lambda c: pl.BlockSpec((seq, SB_HEAD_DIM), lambda b, h: (b, c * N_SB_HEADS + h))
    return pl.pallas_call(
        _sb_prompt_kernel,
        grid=(batch, N_SB_HEADS),
        in_specs=[head(COL_Q), head(COL_K), head(COL_V)],
        out_specs=pl.BlockSpec((seq, SB_HEAD_DIM), lambda b, h: (b, h)),
        out_shape=jax.ShapeDtypeStruct((batch * seq, N_SB_HEADS * SB_HEAD_DIM), BF16),
        scratch_shapes=[pltpu.VMEM((SB_CHAINS, SB_TILE, SB_HEAD_DIM), F32),
                        pltpu.VMEM((SB_CHAINS, SB_TILE, SB_TILE), F32)],
        compiler_params=_params("arbitrary", "arbitrary"),
        name="sb_prompt",
    )(proj, proj, proj)


def _sb_sample_kernel(q_ref, k_ref, v_ref, ck_hbm, cv_hbm, o_ref, kbuf, vbuf, sem, acc_ref, carry_ref,
                      *, cache_row0, past):
    t = SB_TILE
    n_new = q_ref.shape[0]
    n_blocks = past // t
    block_rows = t * N_SB_HEADS
    u = _suffix_matrix()
    base = cache_row0 + pl.program_id(0) * past * N_SB_HEADS

    def copies(kb, slot):
        src = pl.ds(base + kb * block_rows, block_rows)
        return (pltpu.make_async_copy(ck_hbm.at[src, :], kbuf.at[slot], sem.at[0, slot]),
                pltpu.make_async_copy(cv_hbm.at[src, :], vbuf.at[slot], sem.at[1, slot]))

    def start(kb, slot):
        for c in copies(kb, slot):
            c.start()

    def wait(kb, slot):
        for c in copies(kb, slot):
            c.wait()

    slots = kbuf.shape[0]
    ahead = slots - 1
    for kb in range(n_blocks - 1, max(n_blocks - 1 - ahead, -1), -1):
        start(kb, kb % slots)

    causal = (lax.broadcasted_iota(jnp.int32, (n_new, t), 1) < lax.broadcasted_iota(jnp.int32, (n_new, t), 0))
    pad = jnp.zeros((t - n_new, SB_HEAD_DIM), BF16)
    head_cols = [slice(h * SB_HEAD_DIM, (h + 1) * SB_HEAD_DIM) for h in range(N_SB_HEADS)]
    results = _sb_chains([(q_ref[:, cols], jnp.concatenate([k_ref[:, cols], pad], axis=0),
                           jnp.concatenate([v_ref[:, cols], pad], axis=0), 1, None)
                          for cols in head_cols], causal, u)
    live = False
    for h, (pv, carry) in enumerate(results):
        acc_ref[h] = pv
        carry_ref[h] = carry
        live = live | _live(carry)

    def body(state):
        kb, _ = state
        slot = kb % slots
        wait(kb, slot)

        @pl.when(kb >= ahead)
        def _():
            start(kb - ahead, (kb - ahead) % slots)

        head_rows = [pl.ds(h, t, stride=N_SB_HEADS) for h in range(N_SB_HEADS)]
        results = _sb_chains([(q_ref[:, head_cols[h]], kbuf[slot, head_rows[h], :].astype(BF16),
                               vbuf[slot, head_rows[h], :].astype(BF16), 1, carry_ref[h])
                              for h in range(N_SB_HEADS)], None, u)
        live = False
        for h, (pv, carry) in enumerate(results):
            acc_ref[h] += pv
            carry_ref[h] = carry
            live = live | _live(carry)
        return kb - 1, live

    kb_end, _ = lax.while_loop(lambda s: (s[0] >= 0) & s[1], body, (n_blocks - 1, live))

    for back in range(ahead):
        @pl.when(kb_end - back >= 0)
        def _():
            wait(kb_end - back, (kb_end - back) % slots)

    for h in range(N_SB_HEADS):
        o_ref[:, h * SB_HEAD_DIM:(h + 1) * SB_HEAD_DIM] = acc_ref[h].astype(BF16)


def _sb_sample(proj, cache_k, cache_v, *, layer, batch, seq, past):
    assert past % SB_TILE == 0 and seq <= SB_TILE
    blk = lambda c: pl.BlockSpec((seq, CHUNK_COLS), lambda b: (b, c))
    block_rows = SB_TILE * N_SB_HEADS
    return pl.pallas_call(
        functools.partial(_sb_sample_kernel, cache_row0=layer * batch * past * N_SB_HEADS, past=past),
        grid=(batch,),
        in_specs=[blk(COL_Q), blk(COL_K), blk(COL_V),
                  pl.BlockSpec(memory_space=pl.ANY), pl.BlockSpec(memory_space=pl.ANY)],
        out_specs=pl.BlockSpec((seq, N_SB_HEADS * SB_HEAD_DIM), lambda b: (b, 0)),
        out_shape=jax.ShapeDtypeStruct((batch * seq, N_SB_HEADS * SB_HEAD_DIM), BF16),
        scratch_shapes=[pltpu.VMEM((SB_CACHE_SLOTS, block_rows, SB_HEAD_DIM), F32),
                        pltpu.VMEM((SB_CACHE_SLOTS, block_rows, SB_HEAD_DIM), F32),
                        pltpu.SemaphoreType.DMA((2, SB_CACHE_SLOTS)),
                        pltpu.VMEM((N_SB_HEADS, seq, SB_HEAD_DIM), F32),
                        pltpu.VMEM((N_SB_HEADS, seq, SB_TILE), F32)],
        compiler_params=_params("arbitrary"),
        name="sb_sample",
    )(proj, proj, proj, cache_k, cache_v)


def _mixer_kernel(x_ref, osb_ref, u_ref, vn_ref, qm_ref, g0_ref, g1_ref, g2_ref, ws_ref, bs_ref,
                  mk_ref, mv_ref, wb_hbm, wo_hbm, gp_ref, y_ref, osg_ref, omem_ref, wb_ref, wo_ref,
                  stage_ref, sem, *, chunk, seg, cache_rows, layer):
    tm = x_ref.shape[0]

    def fetch(n, slot):
        src = wb_hbm.at[layer, n] if n < N_BRANCH else wo_hbm.at[layer]
        return pltpu.make_async_copy(src, stage_ref.at[slot], sem.at[slot])

    def land(n, first_step):
        if not first_step:
            return
        slot = n % 2
        fetch(n, slot).wait()
        if n < N_BRANCH:
            wb_ref[n] = stage_ref[slot].astype(BF16)
        else:
            wo_ref[...] = stage_ref[slot].astype(BF16)
        if n + 2 <= N_BRANCH:
            fetch(n + 2, slot).start()

    def mem_head(ref, s, h):
        if not cache_rows:
            return ref[s, :, h * MEM_HEAD_DIM:(h + 1) * MEM_HEAD_DIM]
        halves = MEM_HEAD_DIM // SB_HEAD_DIM
        per_token = halves * N_MEM_HEADS
        return jnp.concatenate(
            [ref[pl.ds(s * N_MEM * per_token + c * N_MEM_HEADS + h, N_MEM, stride=per_token), :]
             for c in range(halves)], axis=1).astype(BF16)

    def run(first_step):
        if first_step:
            fetch(0, 0).start()
            fetch(1, 1).start()
        r = lax.broadcasted_iota(jnp.int32, (chunk, chunk), 0)
        c = lax.broadcasted_iota(jnp.int32, (chunk, chunk), 1)
        for g in range(N_SG_GROUPS):
            w_g = jnp.where(c <= r, ws_ref[g], 0.0).astype(BF16)
            b_g = bs_ref[:, g:g + 1]
            cols = slice(g * SG_GROUP_DIM, (g + 1) * SG_GROUP_DIM)
            for n in range(tm // chunk):
                rows = slice(n * chunk, (n + 1) * chunk)
                sg = _dot(w_g, vn_ref[rows, cols]) + b_g
                osg_ref[rows, cols] = (u_ref[rows, cols].astype(F32) * sg).astype(BF16)
        land(0, first_step)
        for s in range(tm // seg):
            rows = slice(s * seg, (s + 1) * seg)
            for h in range(N_MEM_HEADS):
                cols = slice(h * MEM_HEAD_DIM, (h + 1) * MEM_HEAD_DIM)
                sc = _dot_nt(qm_ref[rows, cols], mem_head(mk_ref, s, h)) * (MEM_HEAD_DIM ** -0.5)
                p = jnp.exp(sc - jnp.max(sc, axis=-1, keepdims=True))
                o = _dot(p.astype(BF16), mem_head(mv_ref, s, h)) / jnp.sum(p, axis=-1, keepdims=True)
                omem_ref[rows, cols] = o.astype(BF16)
        for n in range(1, N_BRANCH + 1):
            land(n, first_step)
        merged = (g0_ref[...].astype(F32) * _dot(osb_ref[...], wb_ref[0])
                  + g1_ref[...].astype(F32) * _dot(osg_ref[...], wb_ref[1])
                  + g2_ref[...].astype(F32) * _dot(omem_ref[...], wb_ref[2]))
        out = _dot(merged.astype(BF16), wo_ref[...])
        y_ref[...] = x_ref[...] + _rmsnorm(out, gp_ref[...])

    pl.when(pl.program_id(0) == 0)(lambda: run(True))
    pl.when(pl.program_id(0) > 0)(lambda: run(False))


def _mixer(x, o_sb, proj, w_s, b_s_t, mem_k, mem_v, w_branch, w_out, g_post, *, layer, tm, chunk, seg,
           rows_per_mem, mem_v_first=0, cache_rows=False):
    m = x.shape[0]
    row = lambda i: (i, 0)
    col = lambda cb: pl.BlockSpec((tm, CHUNK_COLS), lambda i: (i, cb))
    mem_blocks = tm // seg
    if cache_rows:
        assert m == tm
        layer_rows = mem_k.shape[0] // w_out.shape[0]
        mem = lambda first: pl.BlockSpec((layer_rows, SB_HEAD_DIM), lambda i: (layer, 0),
                                         pipeline_mode=pl.Buffered(1))
    else:
        mem = lambda first: pl.BlockSpec((mem_blocks, N_MEM, D_MODEL),
                                         lambda i: (first + i * tm // rows_per_mem // mem_blocks, 0, 0))
    const2 = lambda i: (0, 0)
    const3 = lambda i: (0, 0, 0)
    return pl.pallas_call(
        functools.partial(_mixer_kernel, chunk=chunk, seg=seg, cache_rows=cache_rows, layer=layer),
        grid=(m // tm,),
        in_specs=[pl.BlockSpec((tm, D_MODEL), row), pl.BlockSpec((tm, D_MODEL), row),
                  col(COL_U), col(COL_VSG), col(COL_QMEM), col(COL_GATE), col(COL_GATE + 1),
                  col(COL_GATE + 2),
                  pl.BlockSpec((N_SG_GROUPS, chunk, chunk), const3),
                  pl.BlockSpec((chunk, N_SG_GROUPS), const2),
                  mem(0), mem(mem_v_first),
                  pl.BlockSpec(memory_space=pl.ANY),
                  pl.BlockSpec(memory_space=pl.ANY),
                  pl.BlockSpec((1, D_MODEL), const2)],
        out_specs=pl.BlockSpec((tm, D_MODEL), row),
        out_shape=jax.ShapeDtypeStruct((m, D_MODEL), F32),
        scratch_shapes=[pltpu.VMEM((tm, D_MODEL), BF16), pltpu.VMEM((tm, D_MODEL), BF16),
                        pltpu.VMEM((N_BRANCH, D_MODEL, D_MODEL), BF16),
                        pltpu.VMEM((D_MODEL, D_MODEL), BF16),
                        pltpu.VMEM((2, D_MODEL, D_MODEL), F32),
                        pltpu.SemaphoreType.DMA((2,))],
        compiler_params=_params("arbitrary"),
        name="mixer",
    )(x, o_sb, proj, proj, proj, proj, proj, proj, w_s, b_s_t, mem_k, mem_v, w_branch, w_out, g_post)


def _ffn_kernel(x_ref, gpre_ref, wi_hbm, wo_hbm, gpost_ref, y_ref, act_ref, wi_ref, wo_ref,
                stage_a, stage_b, stage_o, sem, *, layer):
    tm = x_ref.shape[0]
    n_chunks = D_FF // FF_CHUNK

    def fetches(n, slot):
        c = n * FF_CHUNK
        return (pltpu.make_async_copy(wi_hbm.at[layer, :, pl.ds(c, FF_CHUNK)], stage_a.at[slot],
                                      sem.at[0, slot]),
                pltpu.make_async_copy(wi_hbm.at[layer, :, pl.ds(D_FF + c, FF_CHUNK)], stage_b.at[slot],
                                      sem.at[1, slot]),
                pltpu.make_async_copy(wo_hbm.at[layer, pl.ds(c, FF_CHUNK), :], stage_o.at[slot],
                                      sem.at[2, slot]))

    def run(first_step):
        for part, r in enumerate(range(0, tm, tm // ROW_PARTS)):
            rows = slice(r, r + tm // ROW_PARTS)
            stream = first_step and part == 0
            if stream:
                for n in range(min(2, n_chunks)):
                    for copy in fetches(n, n):
                        copy.start()
            h = _rmsnorm(x_ref[rows, :], gpre_ref[...]).astype(BF16)
            for n in range(n_chunks):
                c = n * FF_CHUNK
                if stream:
                    slot = n % 2
                    for copy in fetches(n, slot):
                        copy.wait()
                    wi_ref[:, c:c + FF_CHUNK] = stage_a[slot].astype(BF16)
                    wi_ref[:, D_FF + c:D_FF + c + FF_CHUNK] = stage_b[slot].astype(BF16)
                    wo_ref[c:c + FF_CHUNK, :] = stage_o[slot].astype(BF16)
                    if n + 2 < n_chunks:
                        for copy in fetches(n + 2, slot):
                            copy.start()
                a = _dot(h, wi_ref[:, c:c + FF_CHUNK])
                b = _dot(h, wi_ref[:, D_FF + c:D_FF + c + FF_CHUNK])
                act_ref[rows, c:c + FF_CHUNK] = (a * _sigmoid(a) * b).astype(BF16)
            out = _dot(act_ref[rows, :], wo_ref[...])
            y_ref[rows, :] = x_ref[rows, :] + _rmsnorm(out, gpost_ref[...])

    pl.when(pl.program_id(0) == 0)(lambda: run(True))
    pl.when(pl.program_id(0) > 0)(lambda: run(False))


def _ffn(x, g_pre, w_ffn_in, w_ffn_out, g_post, *, layer, tm):
    m = x.shape[0]
    row = lambda i: (i, 0)
    resident = lambda shape: pl.BlockSpec(shape, lambda i: (0, 0), pipeline_mode=pl.Buffered(1))
    return pl.pallas_call(
        functools.partial(_ffn_kernel, layer=layer),
        grid=(m // tm,),
        in_specs=[pl.BlockSpec((tm, D_MODEL), row),
                  resident((1, D_MODEL)),
                  pl.BlockSpec(memory_space=pl.ANY),
                  pl.BlockSpec(memory_space=pl.ANY),
                  resident((1, D_MODEL))],
        out_specs=pl.BlockSpec((tm, D_MODEL), row),
        out_shape=jax.ShapeDtypeStruct((m, D_MODEL), F32),
        scratch_shapes=[pltpu.VMEM((tm, D_FF), BF16),
                        pltpu.VMEM((D_MODEL, 2 * D_FF), BF16),
                        pltpu.VMEM((D_FF, D_MODEL), BF16),
                        pltpu.VMEM((2, D_MODEL, FF_CHUNK), F32),
                        pltpu.VMEM((2, D_MODEL, FF_CHUNK), F32),
                        pltpu.VMEM((2, FF_CHUNK, D_MODEL), F32),
                        pltpu.SemaphoreType.DMA((3, 2))],
        compiler_params=_params("arbitrary"),
        name="ffn",
    )(x, g_pre, w_ffn_in, w_ffn_out, g_post)


def _mem_kv_kernel(mem_ref, g_ref, w_ref, kv32_ref, kv16_ref):
    kv = _dot(_rmsnorm(mem_ref[...], g_ref[...]).astype(BF16), w_ref[...].astype(BF16))
    kv32_ref[0] = kv
    kv16_ref[0] = kv.astype(BF16)


def _mem_kv(mem, g_mem, w_mem_kv, *, layer):
    m = mem.shape[0]
    out = pl.BlockSpec((1, m, D_MODEL), lambda j: (j, 0, 0))
    return pl.pallas_call(
        _mem_kv_kernel,
        grid=(2,),
        in_specs=[pl.BlockSpec((m, D_MODEL), lambda j: (0, 0)),
                  pl.BlockSpec((1, D_MODEL), lambda j: (0, 0)),
                  pl.BlockSpec((None, D_MODEL, D_MODEL), lambda j: (layer, 0, j))],
        out_specs=[out, out],
        out_shape=[jax.ShapeDtypeStruct((2, m, D_MODEL), F32),
                   jax.ShapeDtypeStruct((2, m, D_MODEL), BF16)],
        compiler_params=_params("arbitrary"),
        name="mem_kv",
    )(mem, g_mem, w_mem_kv)


def kernel(x_prompt, x_sample, cache_sb_k, cache_sb_v, cache_mem_k, cache_mem_v, mem_prompt,
           g_pre_mix, w_in, b_gate, ln_sg_g, ln_sg_b, w_spatial, b_spatial, g_mem, w_mem_kv,
           w_branch, w_out, g_post_mix, g_pre_ffn, w_ffn_in, w_ffn_out, g_post_ffn):
    depth = w_in.shape[0]
    batch, seq, _ = x_prompt.shape
    dec_batch, dec_seq, _ = x_sample.shape
    past = cache_sb_k.shape[2]
    tm = 512
    tm_proj = 256
    m_s = dec_batch * dec_seq
    assert (batch * seq) % tm_proj == 0 and seq % tm == 0 and m_s == tm

    xp = x_prompt.reshape(batch * seq, D_MODEL)
    xs = x_sample.reshape(m_s, D_MODEL)
    mem = mem_prompt.reshape(batch * N_MEM, D_MODEL)
    cache_k = cache_sb_k.reshape(-1, SB_HEAD_DIM)
    cache_v = cache_sb_v.reshape(-1, SB_HEAD_DIM)

    def stored_rows(c):
        d, b, n, h, w = c.shape
        halves = w // SB_HEAD_DIM
        return (c.reshape(d, b, n, h, halves, SB_HEAD_DIM).transpose(0, 1, 2, 4, 3, 5)
                .reshape(-1, SB_HEAD_DIM))

    mem_cache_k, mem_cache_v = stored_rows(cache_mem_k), stored_rows(cache_mem_v)
    vec = lambda a: a.reshape(1, -1)
    mem_outs = [[] for _ in range(2)]
    prev_p, prev_s = (), ()
    for l in range(depth):
        proj_args = (vec(g_pre_mix[l]), w_in, vec(b_gate[l]), vec(ln_sg_g[l]), vec(ln_sg_b[l]))
        ffn_args = (vec(g_pre_ffn[l]), w_ffn_in, w_ffn_out, vec(g_post_ffn[l]))

        kv32, kv16 = _mem_kv(mem, vec(g_mem[l]), w_mem_kv, layer=l)
        proj, *prev_p = _in_proj(xp, *proj_args, prev_p, layer=l, depth=depth, tm=tm_proj, emit_vn=False)
        o_sb = _sb_prompt(proj, batch=batch, seq=seq)
        mem_kv16 = kv16.reshape(2 * batch, N_MEM, D_MODEL)
        xp = _mixer(xp, o_sb, proj, w_spatial[l], b_spatial[l].T, mem_kv16, mem_kv16,
                    w_branch, w_out, vec(g_post_mix[l]),
                    layer=l, tm=tm, chunk=SG_CHUNK, seg=tm, rows_per_mem=seq, mem_v_first=batch)
        xp = _ffn(xp, *ffn_args, layer=l, tm=2 * tm)
        mem_outs[0].append(kv32[0].reshape(batch, N_MEM, N_MEM_HEADS, MEM_HEAD_DIM))
        mem_outs[1].append(kv32[1].reshape(batch, N_MEM, N_MEM_HEADS, MEM_HEAD_DIM))

        proj, *prev_s = _in_proj(xs, *proj_args, prev_s, layer=l, depth=depth, tm=tm_proj, emit_vn=True)
        o_sb = _sb_sample(proj, cache_k, cache_v, layer=l, batch=dec_batch, seq=dec_seq, past=past)
        xs = _mixer(xs, o_sb, proj, w_spatial[l][:, :dec_seq, :dec_seq], b_spatial[l][:, :dec_seq].T,
                    mem_cache_k, mem_cache_v, w_branch, w_out, vec(g_post_mix[l]),
                    layer=l, tm=tm, chunk=dec_seq, seg=dec_seq, rows_per_mem=dec_seq, cache_rows=True)
        xs = _ffn(xs, *ffn_args, layer=l, tm=tm)

    heads = (N_SB_HEADS, SB_HEAD_DIM)
    return (xp.reshape(batch, seq, D_MODEL), xs.reshape(dec_batch, dec_seq, D_MODEL),
            prev_p[0].reshape(depth, batch, seq, *heads), prev_p[1].reshape(depth, batch, seq, *heads),
            jnp.stack(mem_outs[0]), jnp.stack(mem_outs[1]),
            prev_s[0].reshape(depth, dec_batch, dec_seq, *heads),
            prev_s[1].reshape(depth, dec_batch, dec_seq, *heads),
            prev_s[2].reshape(depth, dec_batch, dec_seq, D_MODEL))
```

```python
import functools

import jax
import jax.numpy as jnp
from jax import lax
from jax.experimental import pallas as pl
from jax.experimental.pallas import tpu as pltpu

F32 = jnp.float32
BF16 = jnp.bfloat16

D_MODEL = 1024
CHUNK_COLS = 1024
N_SB_HEADS = 8
SB_HEAD_DIM = 128
N_SG_GROUPS = 4
SG_GROUP_DIM = 256
SG_CHUNK = 128
N_MEM = 256
N_MEM_HEADS = 4
MEM_HEAD_DIM = 256
N_BRANCH = 3
D_FF = 2816
EPS = 1e-6

COL_Q, COL_K, COL_V, COL_U, COL_VSG, COL_QMEM, COL_GATE = 0, 1, 2, 3, 4, 5, 6
N_COL_BLOCKS = COL_GATE + N_BRANCH
IN_PROJ_ORDER = (COL_U, COL_Q, COL_VSG, COL_K, COL_GATE, COL_V, COL_GATE + 1, COL_GATE + 2, COL_QMEM)

SB_TILE = 128
SB_WINDOW = 3
SB_CACHE_SLOTS = 4
SB_CHAINS = 16
SB_LOG2_WEIGHT_FLOOR = -150.04
ROW_PARTS = 2
FF_CHUNK = 256
LOG2_E = 1.4426950408889634

VMEM_LIMIT = 56 * 1024 * 1024


def _dot(a, b):
    return jnp.dot(a, b, preferred_element_type=F32)


def _dot_nt(a, b):
    return lax.dot_general(a, b, (((1,), (1,)), ((), ())), preferred_element_type=F32)


def _rmsnorm(x, g):
    return x * lax.rsqrt(jnp.mean(x * x, axis=-1, keepdims=True) + EPS) * g


def _gelu(x):
    c, k = 0.7978845608028654, 0.044715
    half = 0.5 * x
    return half + half * jnp.tanh(x * (c + (c * k) * (x * x)))


def _sigmoid(x):
    return 0.5 + 0.5 * jnp.tanh(0.5 * x)


def _layer_weight(layer, *shape):
    zeros = (0,) * len(shape)
    return pl.BlockSpec((None, *shape), lambda *_: (layer, *zeros), pipeline_mode=pl.Buffered(1))


def _params(*semantics):
    return pltpu.CompilerParams(dimension_semantics=semantics, vmem_limit_bytes=VMEM_LIMIT)


def _store_heads(dst_ref, acc):
    tm = acc.shape[0]
    for h in range(N_SB_HEADS):
        head = acc[:, h * SB_HEAD_DIM:(h + 1) * SB_HEAD_DIM]
        dst_ref[0, pl.ds(h, tm, stride=N_SB_HEADS), :] = head
        for r in range(1, dst_ref.shape[0]):
            dst_ref[r, pl.ds(h * tm, tm), :] = head


def _in_proj_kernel(*refs, n_prev):
    (x_ref, g_ref, w_ref, bg_ref, lng_ref, lnb_ref), refs = refs[:6], refs[6 + n_prev:]
    proj_ref, k_ref, v_ref, vn_ref = refs
    h = _rmsnorm(x_ref[...], g_ref[...]).astype(BF16)
    for j in IN_PROJ_ORDER:
        cols = slice(j * CHUNK_COLS, (j + 1) * CHUNK_COLS)
        acc = _dot(h, w_ref[:, cols])
        if j == COL_K:
            _store_heads(k_ref, acc)
        elif j == COL_V:
            _store_heads(v_ref, acc)
        elif j == COL_U:
            acc = _gelu(acc)
        elif j == COL_VSG:
            a = _gelu(acc)
            mu = jnp.mean(a, axis=-1, keepdims=True)
            d = a - mu
            var = jnp.mean(d * d, axis=-1, keepdims=True)
            acc = d * lax.rsqrt(var + EPS) * lng_ref[...] + lnb_ref[...]
            if vn_ref is not None:
                for r in range(vn_ref.shape[0]):
                    vn_ref[r] = acc
        elif j >= COL_GATE:
            g = slice((j - COL_GATE) * CHUNK_COLS, (j - COL_GATE + 1) * CHUNK_COLS)
            acc = _sigmoid(acc + bg_ref[:, g])
        proj_ref[:, cols] = acc.astype(BF16)


def _in_proj(x, g_pre, w_in, b_gate, ln_g, ln_b, prev, *, layer, depth, tm, emit_vn):
    m = x.shape[0]
    n_i = m // tm
    d_in = N_COL_BLOCKS * CHUNK_COLS
    row = lambda i: (i, 0)
    resident = lambda shape: pl.BlockSpec(shape, lambda i: (0, 0), pipeline_mode=pl.Buffered(1))
    layers = depth if layer == 0 else 1
    layer_rows = lambda shape: pl.BlockSpec((layers, *shape), lambda i: (layer, i, 0))
    out_shape = [jax.ShapeDtypeStruct((m, d_in), BF16),
                 jax.ShapeDtypeStruct((depth, m * N_SB_HEADS, SB_HEAD_DIM), F32),
                 jax.ShapeDtypeStruct((depth, m * N_SB_HEADS, SB_HEAD_DIM), F32)]
    out_specs = [pl.BlockSpec((tm, d_in), row),
                 layer_rows((tm * N_SB_HEADS, SB_HEAD_DIM)),
                 layer_rows((tm * N_SB_HEADS, SB_HEAD_DIM))]
    if emit_vn:
        out_shape.append(jax.ShapeDtypeStruct((depth, m, CHUNK_COLS), F32))
        out_specs.append(layer_rows((tm, CHUNK_COLS)))
    n_prev = len(prev)
    assert n_prev == (0 if layer == 0 else len(out_shape) - 1)

    def body(*refs):
        if not emit_vn:
            refs = refs + (None,)
        _in_proj_kernel(*refs, n_prev=n_prev)

    return pl.pallas_call(
        body,
        grid=(n_i,),
        in_specs=[pl.BlockSpec((tm, D_MODEL), row),
                  resident((1, D_MODEL)),
                  _layer_weight(layer, D_MODEL, d_in),
                  resident((1, N_BRANCH * CHUNK_COLS)),
                  resident((1, CHUNK_COLS)),
                  resident((1, CHUNK_COLS))]
                 + [pl.BlockSpec(memory_space=pl.ANY)] * n_prev,
        out_specs=out_specs,
        out_shape=out_shape,
        input_output_aliases={6 + p: 1 + p for p in range(n_prev)},
        compiler_params=_params("arbitrary"),
        name="in_proj",
    )(x, g_pre, w_in, b_gate, ln_g, ln_b, *prev)


def _suffix_matrix():
    t = SB_TILE
    j = lax.broadcasted_iota(jnp.int32, (2 * t, 2 * t), 0) % t
    s = lax.broadcasted_iota(jnp.int32, (2 * t, 2 * t), 1)
    return jnp.where((j > s) | (s >= t), 1.0, 0.0).astype(BF16)


def _sb_front(qk, mask):
    zns = [_dot_nt(q, k) * (-(SB_HEAD_DIM ** -0.5) * LOG2_E) for q, k, _ in qk]
    keeps = [jnp.minimum(zn, 0.0) - jnp.log2(1.0 + jnp.exp2(-jnp.abs(zn))) for zn in zns]
    betas = [keep - zn for keep, zn in zip(keeps, zns)]
    pieces = []
    for keep, (_, _, n_tiles) in zip(keeps, qk):
        per_tile = []
        for d in range(n_tiles):
            log_keep = keep[:, d * SB_TILE:(d + 1) * SB_TILE]
            if mask is not None and d == n_tiles - 1:
                log_keep = jnp.where(mask, log_keep, 0.0)
            hi = log_keep.astype(BF16)
            lo = (log_keep - hi.astype(F32)).astype(BF16)
            per_tile.append(jnp.concatenate([hi, lo], axis=1))
        pieces.append(per_tile)
    return betas, pieces


def _sb_back(betas, pieces, vs, carries, mask, u):
    sums = [[_dot(p, u) for p in per_tile] for per_tile in pieces]
    ws = []
    for beta, per_tile, carry in zip(betas, sums, carries):
        n_tiles = len(per_tile)
        w_tiles = [None] * n_tiles
        for d in reversed(range(n_tiles)):
            suffix, total = per_tile[d][:, :SB_TILE], per_tile[d][:, SB_TILE:]
            between = suffix if carry is None else carry + suffix
            w = jnp.exp2(beta[:, d * SB_TILE:(d + 1) * SB_TILE] + between)
            if mask is not None and d == n_tiles - 1:
                w = jnp.where(mask, w, 0.0)
            w_tiles[d] = w.astype(BF16)
            carry = total if carry is None else carry + total
        ws.append((jnp.concatenate(w_tiles, axis=1), carry))
    return [(_dot(w, v), carry) for (w, carry), v in zip(ws, vs)]


def _sb_chains(chains, mask, u):
    betas, pieces = _sb_front([(q, k, n) for q, k, _, n, _ in chains], mask)
    return _sb_back(betas, pieces, [c[2] for c in chains], [c[4] for c in chains], mask, u)


def _live(carry):
    return jnp.max(carry) >= SB_LOG2_WEIGHT_FLOOR


def _sb_prompt_kernel(q_ref, k_ref, v_ref, o_ref, acc_ref, carry_ref):
    t = SB_TILE
    u = _suffix_matrix()
    causal = (lax.broadcasted_iota(jnp.int32, (t, t), 1) < lax.broadcasted_iota(jnp.int32, (t, t), 0))

    def rows(tile, n=1):
        start = tile * t
        return pl.ds(start if isinstance(start, int) else pl.multiple_of(start, t), n * t)

    def keys(qt, n_tiles):
        return rows(qt - (n_tiles - 1), n_tiles)

    def settle(qt, sizes, results):
        def pending(j, carries):
            worst = None
            for s, n in enumerate(sizes):
                c = jnp.where(qt + s - n - j >= 0, carries[s], -jnp.inf)
                worst = c if worst is None else jnp.maximum(worst, c)
            return _live(worst)

        for s, (pv, carry) in enumerate(results):
            acc_ref[s] = pv
            carry_ref[s] = carry
            o_ref[rows(qt + s), :] = pv.astype(BF16)

        def body(state):
            j, _ = state
            kbs = [qt + s - n - j for s, n in enumerate(sizes)]
            tiles = [rows(jnp.maximum(kb, 0)) for kb in kbs]
            stepped = _sb_chains([(q_ref[rows(qt + s), :], k_ref[tiles[s], :], v_ref[tiles[s], :], 1,
                                   carry_ref[s]) for s in range(len(sizes))], None, u)
            carries = []
            for s, (pv, carry) in enumerate(stepped):
                acc = acc_ref[s] + jnp.where(kbs[s] >= 0, pv, 0.0)
                carry = jnp.where(kbs[s] >= 0, carry, carry_ref[s])
                acc_ref[s] = acc
                carry_ref[s] = carry
                o_ref[rows(qt + s), :] = acc.astype(BF16)
                carries.append(carry)
            return j + 1, pending(j + 1, carries)

        lax.while_loop(lambda state: state[1], body, (0, pending(0, [c for _, c in results])))

    def tile_group(qt, sizes):
        settle(qt, sizes, _sb_chains(
            [(q_ref[rows(qt + s), :], k_ref[keys(qt + s, n), :], v_ref[keys(qt + s, n), :], n, None)
             for s, n in enumerate(sizes)], causal, u))

    n_q = q_ref.shape[0] // t
    first = -(-(SB_WINDOW - 1) // SB_CHAINS) * SB_CHAINS
    for qt in range(0, first, SB_CHAINS):
        tile_group(qt, [min(qt + s + 1, SB_WINDOW) for s in range(SB_CHAINS)])

    def step(i, _):
        tile_group(first + SB_CHAINS * i, [SB_WINDOW] * SB_CHAINS)
        return 0

    lax.fori_loop(0, (n_q - first) // SB_CHAINS, step, 0)


def _sb_prompt(proj, *, batch, seq):
    assert seq % (SB_CHAINS * SB_TILE) == 0
    head = lambda c: pl.BlockSpec((seq, SB_HEAD_DIM), lambda b, h: (b, c * N_SB_HEADS + h))
    return pl.pallas_call(
        _sb_prompt_kernel,
        grid=(batch, N_SB_HEADS),
        in_specs=[head(COL_Q), head(COL_K), head(COL_V)],
        out_specs=pl.BlockSpec((seq, SB_HEAD_DIM), lambda b, h: (b, h)),
        out_shape=jax.ShapeDtypeStruct((batch * seq, N_SB_HEADS * SB_HEAD_DIM), BF16),
        scratch_shapes=[pltpu.VMEM((SB_CHAINS, SB_TILE, SB_HEAD_DIM), F32),
                        pltpu.VMEM((SB_CHAINS, SB_TILE, SB_TILE), F32)],
        compiler_params=_params("arbitrary", "arbitrary"),
        name="sb_prompt",
    )(proj, proj, proj)


def _sb_sample_kernel(q_ref, k_ref, v_ref, ck_hbm, cv_hbm, o_ref, kbuf, vbuf, sem, acc_ref, carry_ref,
                      *, cache_row0, past):
    t = SB_TILE
    n_new = q_ref.shape[0]
    n_blocks = past // t
    block_rows = t * N_SB_HEADS
    u = _suffix_matrix()
    base = cache_row0 + pl.program_id(0) * past * N_SB_HEADS

    def copies(kb, slot):
        src = pl.ds(base + kb * block_rows, block_rows)
        return (pltpu.make_async_copy(ck_hbm.at[src, :], kbuf.at[slot], sem.at[0, slot]),
                pltpu.make_async_copy(cv_hbm.at[src, :], vbuf.at[slot], sem.at[1, slot]))

    def start(kb, slot):
        for c in copies(kb, slot):
            c.start()

    def wait(kb, slot):
        for c in copies(kb, slot):
            c.wait()

    slots = kbuf.shape[0]
    ahead = slots - 1
    for kb in range(n_blocks - 1, max(n_blocks - 1 - ahead, -1), -1):
        start(kb, kb % slots)

    causal = (lax.broadcasted_iota(jnp.int32, (n_new, t), 1) < lax.broadcasted_iota(jnp.int32, (n_new, t), 0))
    pad = jnp.zeros((t - n_new, SB_HEAD_DIM), BF16)
    head_cols = [slice(h * SB_HEAD_DIM, (h + 1) * SB_HEAD_DIM) for h in range(N_SB_HEADS)]
    results = _sb_chains([(q_ref[:, cols], jnp.concatenate([k_ref[:, cols], pad], axis=0),
                           jnp.concatenate([v_ref[:, cols], pad], axis=0), 1, None)
                          for cols in head_cols], causal, u)
    live = False
    for h, (pv, carry) in enumerate(results):
        acc_ref[h] = pv
        carry_ref[h] = carry
        live = live | _live(carry)

    def body(state):
        kb, _ = state
        slot = kb % slots
        wait(kb, slot)

        @pl.when(kb >= ahead)
        def _():
            start(kb - ahead, (kb - ahead) % slots)

        head_rows = [pl.ds(h, t, stride=N_SB_HEADS) for h in range(N_SB_HEADS)]
        results = _sb_chains([(q_ref[:, head_cols[h]], kbuf[slot, head_rows[h], :].astype(BF16),
                               vbuf[slot, head_rows[h], :].astype(BF16), 1, carry_ref[h])
                              for h in range(N_SB_HEADS)], None, u)
        live = False
        for h, (pv, carry) in enumerate(results):
            acc_ref[h] += pv
            carry_ref[h] = carry
            live = live | _live(carry)
        return kb - 1, live

    kb_end, _ = lax.while_loop(lambda s: (s[0] >= 0) & s[1], body, (n_blocks - 1, live))

    for back in range(ahead):
        @pl.when(kb_end - back >= 0)
        def _():
            wait(kb_end - back, (kb_end - back) % slots)

    for h in range(N_SB_HEADS):
        o_ref[:, h * SB_HEAD_DIM:(h + 1) * SB_HEAD_DIM] = acc_ref[h].astype(BF16)


def _sb_sample(proj, cache_k, cache_v, *, layer, batch, seq, past):
    assert past % SB_TILE == 0 and seq <= SB_TILE
    blk = lambda c: pl.BlockSpec((seq, CHUNK_COLS), lambda b: (b, c))
    block_rows = SB_TILE * N_SB_HEADS
    return pl.pallas_call(
        functools.partial(_sb_sample_kernel, cache_row0=layer * batch * past * N_SB_HEADS, past=past),
        grid=(batch,),
        in_specs=[blk(COL_Q), blk(COL_K), blk(COL_V),
                  pl.BlockSpec(memory_space=pl.ANY), pl.BlockSpec(memory_space=pl.ANY)],
        out_specs=pl.BlockSpec((seq, N_SB_HEADS * SB_HEAD_DIM), lambda b: (b, 0)),
        out_shape=jax.ShapeDtypeStruct((batch * seq, N_SB_HEADS * SB_HEAD_DIM), BF16),
        scratch_shapes=[pltpu.VMEM((SB_CACHE_SLOTS, block_rows, SB_HEAD_DIM), F32),
                        pltpu.VMEM((SB_CACHE_SLOTS, block_rows, SB_HEAD_DIM), F32),
                        pltpu.SemaphoreType.DMA((2, SB_CACHE_SLOTS)),
                        pltpu.VMEM((N_SB_HEADS, seq, SB_HEAD_DIM), F32),
                        pltpu.VMEM((N_SB_HEADS, seq, SB_TILE), F32)],
        compiler_params=_params("arbitrary"),
        name="sb_sample",
    )(proj, proj, proj, cache_k, cache_v)


def _mixer_kernel(x_ref, osb_ref, u_ref, vn_ref, qm_ref, g0_ref, g1_ref, g2_ref, ws_ref, bs_ref,
                  mk_ref, mv_ref, wb_ref, wo_ref, gp_ref, y_ref, osg_ref, omem_ref, *, chunk, seg,
                  cache_rows):
    tm = x_ref.shape[0]

    def mem_head(ref, s, h):
        if not cache_rows:
            return ref[s, :, h * MEM_HEAD_DIM:(h + 1) * MEM_HEAD_DIM]
        halves = MEM_HEAD_DIM // SB_HEAD_DIM
        per_token = halves * N_MEM_HEADS
        return jnp.concatenate(
            [ref[pl.ds(s * N_MEM * per_token + c * N_MEM_HEADS + h, N_MEM, stride=per_token), :]
             for c in range(halves)], axis=1).astype(BF16)

    r = lax.broadcasted_iota(jnp.int32, (chunk, chunk), 0)
    c = lax.broadcasted_iota(jnp.int32, (chunk, chunk), 1)
    for g in range(N_SG_GROUPS):
        w_g = jnp.where(c <= r, ws_ref[g], 0.0).astype(BF16)
        b_g = bs_ref[:, g:g + 1]
        cols = slice(g * SG_GROUP_DIM, (g + 1) * SG_GROUP_DIM)
        for n in range(tm // chunk):
            rows = slice(n * chunk, (n + 1) * chunk)
            sg = _dot(w_g, vn_ref[rows, cols]) + b_g
            osg_ref[rows, cols] = (u_ref[rows, cols].astype(F32) * sg).astype(BF16)
    for s in range(tm // seg):
        rows = slice(s * seg, (s + 1) * seg)
        for h in range(N_MEM_HEADS):
            cols = slice(h * MEM_HEAD_DIM, (h + 1) * MEM_HEAD_DIM)
            sc = _dot_nt(qm_ref[rows, cols], mem_head(mk_ref, s, h)) * (MEM_HEAD_DIM ** -0.5)
            p = jnp.exp(sc - jnp.max(sc, axis=-1, keepdims=True))
            o = _dot(p.astype(BF16), mem_head(mv_ref, s, h)) / jnp.sum(p, axis=-1, keepdims=True)
            omem_ref[rows, cols] = o.astype(BF16)
    merged = (g0_ref[...].astype(F32) * _dot(osb_ref[...], wb_ref[0])
              + g1_ref[...].astype(F32) * _dot(osg_ref[...], wb_ref[1])
              + g2_ref[...].astype(F32) * _dot(omem_ref[...], wb_ref[2]))
    out = _dot(merged.astype(BF16), wo_ref[...])
    y_ref[...] = x_ref[...] + _rmsnorm(out, gp_ref[...])


def _mixer(x, o_sb, proj, w_s, b_s_t, mem_k, mem_v, w_branch, w_out, g_post, *, layer, tm, chunk, seg,
           rows_per_mem, mem_v_first=0, cache_rows=False):
    m = x.shape[0]
    row = lambda i: (i, 0)
    col = lambda cb: pl.BlockSpec((tm, CHUNK_COLS), lambda i: (i, cb))
    mem_blocks = tm // seg
    if cache_rows:
        assert m == tm
        layer_rows = mem_k.shape[0] // w_out.shape[0]
        mem = lambda first: pl.BlockSpec((layer_rows, SB_HEAD_DIM), lambda i: (layer, 0),
                                         pipeline_mode=pl.Buffered(1))
    else:
        mem = lambda first: pl.BlockSpec((mem_blocks, N_MEM, D_MODEL),
                                         lambda i: (first + i * tm // rows_per_mem // mem_blocks, 0, 0))
    const2 = lambda i: (0, 0)
    const3 = lambda i: (0, 0, 0)
    return pl.pallas_call(
        functools.partial(_mixer_kernel, chunk=chunk, seg=seg, cache_rows=cache_rows),
        grid=(m // tm,),
        in_specs=[pl.BlockSpec((tm, D_MODEL), row), pl.BlockSpec((tm, D_MODEL), row),
                  col(COL_U), col(COL_VSG), col(COL_QMEM), col(COL_GATE), col(COL_GATE + 1),
                  col(COL_GATE + 2),
                  pl.BlockSpec((N_SG_GROUPS, chunk, chunk), const3),
                  pl.BlockSpec((chunk, N_SG_GROUPS), const2),
                  mem(0), mem(mem_v_first),
                  _layer_weight(layer, N_BRANCH, D_MODEL, D_MODEL),
                  _layer_weight(layer, D_MODEL, D_MODEL),
                  pl.BlockSpec((1, D_MODEL), const2)],
        out_specs=pl.BlockSpec((tm, D_MODEL), row),
        out_shape=jax.ShapeDtypeStruct((m, D_MODEL), F32),
        scratch_shapes=[pltpu.VMEM((tm, D_MODEL), BF16), pltpu.VMEM((tm, D_MODEL), BF16)],
        compiler_params=_params("arbitrary"),
        name="mixer",
    )(x, o_sb, proj, proj, proj, proj, proj, proj, w_s, b_s_t, mem_k, mem_v, w_branch, w_out, g_post)


def _ffn_kernel(x_ref, gpre_ref, wi_ref, wo_ref, gpost_ref, y_ref, act_ref):
    tm = x_ref.shape[0]
    for r in range(0, tm, tm // ROW_PARTS):
        rows = slice(r, r + tm // ROW_PARTS)
        h = _rmsnorm(x_ref[rows, :], gpre_ref[...]).astype(BF16)
        for c in range(0, D_FF, FF_CHUNK):
            a = _dot(h, wi_ref[:, c:c + FF_CHUNK])
            b = _dot(h, wi_ref[:, D_FF + c:D_FF + c + FF_CHUNK])
            act_ref[rows, c:c + FF_CHUNK] = (a * _sigmoid(a) * b).astype(BF16)
        out = _dot(act_ref[rows, :], wo_ref[...])
        y_ref[rows, :] = x_ref[rows, :] + _rmsnorm(out, gpost_ref[...])


def _ffn(x, g_pre, w_ffn_in, w_ffn_out, g_post, *, layer, tm):
    m = x.shape[0]
    row = lambda i: (i, 0)
    resident = lambda shape: pl.BlockSpec(shape, lambda i: (0, 0), pipeline_mode=pl.Buffered(1))
    return pl.pallas_call(
        _ffn_kernel,
        grid=(m // tm,),
        in_specs=[pl.BlockSpec((tm, D_MODEL), row),
                  resident((1, D_MODEL)),
                  _layer_weight(layer, D_MODEL, 2 * D_FF),
                  _layer_weight(layer, D_FF, D_MODEL),
                  resident((1, D_MODEL))],
        out_specs=pl.BlockSpec((tm, D_MODEL), row),
        out_shape=jax.ShapeDtypeStruct((m, D_MODEL), F32),
        scratch_shapes=[pltpu.VMEM((tm, D_FF), BF16)],
        compiler_params=_params("arbitrary"),
        name="ffn",
    )(x, g_pre, w_ffn_in, w_ffn_out, g_post)


def _mem_kv_kernel(mem_ref, g_ref, w_ref, kv32_ref, kv16_ref):
    kv = _dot(_rmsnorm(mem_ref[...], g_ref[...]).astype(BF16), w_ref[...].astype(BF16))
    kv32_ref[0] = kv
    kv16_ref[0] = kv.astype(BF16)


def _mem_kv(mem, g_mem, w_mem_kv, *, layer):
    m = mem.shape[0]
    out = pl.BlockSpec((1, m, D_MODEL), lambda j: (j, 0, 0))
    return pl.pallas_call(
        _mem_kv_kernel,
        grid=(2,),
        in_specs=[pl.BlockSpec((m, D_MODEL), lambda j: (0, 0)),
                  pl.BlockSpec((1, D_MODEL), lambda j: (0, 0)),
                  pl.BlockSpec((None, D_MODEL, D_MODEL), lambda j: (layer, 0, j))],
        out_specs=[out, out],
        out_shape=[jax.ShapeDtypeStruct((2, m, D_MODEL), F32),
                   jax.ShapeDtypeStruct((2, m, D_MODEL), BF16)],
        compiler_params=_params("arbitrary"),
        name="mem_kv",
    )(mem, g_mem, w_mem_kv)


def kernel(x_prompt, x_sample, cache_sb_k, cache_sb_v, cache_mem_k, cache_mem_v, mem_prompt,
           g_pre_mix, w_in, b_gate, ln_sg_g, ln_sg_b, w_spatial, b_spatial, g_mem, w_mem_kv,
           w_branch, w_out, g_post_mix, g_pre_ffn, w_ffn_in, w_ffn_out, g_post_ffn):
    depth = w_in.shape[0]
    batch, seq, _ = x_prompt.shape
    dec_batch, dec_seq, _ = x_sample.shape
    past = cache_sb_k.shape[2]
    tm = 512
    tm_proj = 256
    m_s = dec_batch * dec_seq
    assert (batch * seq) % tm_proj == 0 and seq % tm == 0 and m_s == tm

    xp = x_prompt.reshape(batch * seq, D_MODEL)
    xs = x_sample.reshape(m_s, D_MODEL)
    mem = mem_prompt.reshape(batch * N_MEM, D_MODEL)
    cache_k = cache_sb_k.reshape(-1, SB_HEAD_DIM)
    cache_v = cache_sb_v.reshape(-1, SB_HEAD_DIM)

    def stored_rows(c):
        d, b, n, h, w = c.shape
        halves = w // SB_HEAD_DIM
        return (c.reshape(d, b, n, h, halves, SB_HEAD_DIM).transpose(0, 1, 2, 4, 3, 5)
                .reshape(-1, SB_HEAD_DIM))

    mem_cache_k, mem_cache_v = stored_rows(cache_mem_k), stored_rows(cache_mem_v)
    vec = lambda a: a.reshape(1, -1)
    mem_outs = [[] for _ in range(2)]
    prev_p, prev_s = (), ()
    w_in, w_branch, w_out, w_ffn_in, w_ffn_out = (
        w.astype(BF16) for w in (w_in, w_branch, w_out, w_ffn_in, w_ffn_out))
    for l in range(depth):
        proj_args = (vec(g_pre_mix[l]), w_in, vec(b_gate[l]), vec(ln_sg_g[l]), vec(ln_sg_b[l]))
        ffn_args = (vec(g_pre_ffn[l]), w_ffn_in, w_ffn_out, vec(g_post_ffn[l]))

        kv32, kv16 = _mem_kv(mem, vec(g_mem[l]), w_mem_kv, layer=l)
        proj, *prev_p = _in_proj(xp, *proj_args, prev_p, layer=l, depth=depth, tm=tm_proj, emit_vn=False)
        o_sb = _sb_prompt(proj, batch=batch, seq=seq)
        mem_kv16 = kv16.reshape(2 * batch, N_MEM, D_MODEL)
        xp = _mixer(xp, o_sb, proj, w_spatial[l], b_spatial[l].T, mem_kv16, mem_kv16,
                    w_branch, w_out, vec(g_post_mix[l]),
                    layer=l, tm=tm, chunk=SG_CHUNK, seg=tm, rows_per_mem=seq, mem_v_first=batch)
        xp = _ffn(xp, *ffn_args, layer=l, tm=2 * tm)
        mem_outs[0].append(kv32[0].reshape(batch, N_MEM, N_MEM_HEADS, MEM_HEAD_DIM))
        mem_outs[1].append(kv32[1].reshape(batch, N_MEM, N_MEM_HEADS, MEM_HEAD_DIM))

        proj, *prev_s = _in_proj(xs, *proj_args, prev_s, layer=l, depth=depth, tm=tm_proj, emit_vn=True)
        o_sb = _sb_sample(proj, cache_k, cache_v, layer=l, batch=dec_batch, seq=dec_seq, past=past)
        xs = _mixer(xs, o_sb, proj, w_spatial[l][:, :dec_seq, :dec_seq], b_spatial[l][:, :dec_seq].T,
                    mem_cache_k, mem_cache_v, w_branch, w_out, vec(g_post_mix[l]),
                    layer=l, tm=tm, chunk=dec_seq, seg=dec_seq, rows_per_mem=dec_seq, cache_rows=True)
        xs = _ffn(xs, *ffn_args, layer=l, tm=tm)

    heads = (N_SB_HEADS, SB_HEAD_DIM)
    return (xp.reshape(batch, seq, D_MODEL), xs.reshape(dec_batch, dec_seq, D_MODEL),
            prev_p[0].reshape(depth, batch, seq, *heads), prev_p[1].reshape(depth, batch, seq, *heads),
            jnp.stack(mem_outs[0]), jnp.stack(mem_outs[1]),
            prev_s[0].reshape(depth, dec_batch, dec_seq, *heads),
            prev_s[1].reshape(depth, dec_batch, dec_seq, *heads),
            prev_s[2].reshape(depth, dec_batch, dec_seq, D_MODEL))
```

```python
import functools

import jax
import jax.numpy as jnp
from jax import lax
from jax.experimental import pallas as pl
from jax.experimental.pallas import tpu as pltpu

F32 = jnp.float32
BF16 = jnp.bfloat16

D_MODEL = 1024
CHUNK_COLS = 1024
N_SB_HEADS = 8
SB_HEAD_DIM = 128
N_SG_GROUPS = 4
SG_GROUP_DIM = 256
SG_CHUNK = 128
N_MEM = 256
N_MEM_HEADS = 4
MEM_HEAD_DIM = 256
N_BRANCH = 3
D_FF = 2816
EPS = 1e-6

COL_Q, COL_K, COL_V, COL_U, COL_VSG, COL_QMEM, COL_GATE = 0, 1, 2, 3, 4, 5, 6
N_COL_BLOCKS = COL_GATE + N_BRANCH
IN_PROJ_ORDER = (COL_U, COL_Q, COL_VSG, COL_K, COL_GATE, COL_V, COL_GATE + 1, COL_GATE + 2, COL_QMEM)

SB_TILE = 128
SB_WINDOW = 3
SB_CACHE_SLOTS = 4
SB_CHAINS = 16
SB_LOG2_WEIGHT_FLOOR = -150.04
ROW_PARTS = 2
FF_CHUNK = 256
LOG2_E = 1.4426950408889634

VMEM_LIMIT = 56 * 1024 * 1024


def _dot(a, b):
    return jnp.dot(a, b, preferred_element_type=F32)


def _dot_nt(a, b):
    return lax.dot_general(a, b, (((1,), (1,)), ((), ())), preferred_element_type=F32)


def _rmsnorm(x, g):
    return x * lax.rsqrt(jnp.mean(x * x, axis=-1, keepdims=True) + EPS) * g


def _gelu(x):
    c, k = 0.7978845608028654, 0.044715
    half = 0.5 * x
    return half + half * jnp.tanh(x * (c + (c * k) * (x * x)))


def _sigmoid(x):
    return 0.5 + 0.5 * jnp.tanh(0.5 * x)


def _layer_weight(layer, *shape):
    zeros = (0,) * len(shape)
    return pl.BlockSpec((None, *shape), lambda *_: (layer, *zeros), pipeline_mode=pl.Buffered(1))


def _params(*semantics):
    return pltpu.CompilerParams(dimension_semantics=semantics, vmem_limit_bytes=VMEM_LIMIT)


def _store_heads(dst_ref, acc):
    tm = acc.shape[0]
    for h in range(N_SB_HEADS):
        head = acc[:, h * SB_HEAD_DIM:(h + 1) * SB_HEAD_DIM]
        dst_ref[0, pl.ds(h, tm, stride=N_SB_HEADS), :] = head
        for r in range(1, dst_ref.shape[0]):
            dst_ref[r, pl.ds(h * tm, tm), :] = head


def _in_proj_kernel(*refs, n_prev):
    (x_ref, g_ref, w_ref, bg_ref, lng_ref, lnb_ref), refs = refs[:6], refs[6 + n_prev:]
    proj_ref, k_ref, v_ref, vn_ref = refs
    h = _rmsnorm(x_ref[...], g_ref[...]).astype(BF16)
    for j in IN_PROJ_ORDER:
        cols = slice(j * CHUNK_COLS, (j + 1) * CHUNK_COLS)
        acc = _dot(h, w_ref[:, cols])
        if j == COL_K:
            _store_heads(k_ref, acc)
        elif j == COL_V:
            _store_heads(v_ref, acc)
        elif j == COL_U:
            acc = _gelu(acc)
        elif j == COL_VSG:
            a = _gelu(acc)
            mu = jnp.mean(a, axis=-1, keepdims=True)
            d = a - mu
            var = jnp.mean(d * d, axis=-1, keepdims=True)
            acc = d * lax.rsqrt(var + EPS) * lng_ref[...] + lnb_ref[...]
            if vn_ref is not None:
                for r in range(vn_ref.shape[0]):
                    vn_ref[r] = acc
        elif j >= COL_GATE:
            g = slice((j - COL_GATE) * CHUNK_COLS, (j - COL_GATE + 1) * CHUNK_COLS)
            acc = _sigmoid(acc + bg_ref[:, g])
        proj_ref[:, cols] = acc.astype(BF16)


def _in_proj(x, g_pre, w_in, b_gate, ln_g, ln_b, prev, *, layer, depth, tm, emit_vn):
    m = x.shape[0]
    n_i = m // tm
    d_in = N_COL_BLOCKS * CHUNK_COLS
    row = lambda i: (i, 0)
    resident = lambda shape: pl.BlockSpec(shape, lambda i: (0, 0), pipeline_mode=pl.Buffered(1))
    layers = depth if layer == 0 else 1
    layer_rows = lambda shape: pl.BlockSpec((layers, *shape), lambda i: (layer, i, 0))
    out_shape = [jax.ShapeDtypeStruct((m, d_in), BF16),
                 jax.ShapeDtypeStruct((depth, m * N_SB_HEADS, SB_HEAD_DIM), F32),
                 jax.ShapeDtypeStruct((depth, m * N_SB_HEADS, SB_HEAD_DIM), F32)]
    out_specs = [pl.BlockSpec((tm, d_in), row),
                 layer_rows((tm * N_SB_HEADS, SB_HEAD_DIM)),
                 layer_rows((tm * N_SB_HEADS, SB_HEAD_DIM))]
    if emit_vn:
        out_shape.append(jax.ShapeDtypeStruct((depth, m, CHUNK_COLS), F32))
        out_specs.append(layer_rows((tm, CHUNK_COLS)))
    n_prev = len(prev)
    assert n_prev == (0 if layer == 0 else len(out_shape) - 1)

    def body(*refs):
        if not emit_vn:
            refs = refs + (None,)
        _in_proj_kernel(*refs, n_prev=n_prev)

    return pl.pallas_call(
        body,
        grid=(n_i,),
        in_specs=[pl.BlockSpec((tm, D_MODEL), row),
                  resident((1, D_MODEL)),
                  _layer_weight(layer, D_MODEL, d_in),
                  resident((1, N_BRANCH * CHUNK_COLS)),
                  resident((1, CHUNK_COLS)),
                  resident((1, CHUNK_COLS))]
                 + [pl.BlockSpec(memory_space=pl.ANY)] * n_prev,
        out_specs=out_specs,
        out_shape=out_shape,
        input_output_aliases={6 + p: 1 + p for p in range(n_prev)},
        compiler_params=_params("arbitrary"),
        name="in_proj",
    )(x, g_pre, w_in, b_gate, ln_g, ln_b, *prev)


def _suffix_matrix():
    t = SB_TILE
    j = lax.broadcasted_iota(jnp.int32, (2 * t, 2 * t), 0) % t
    s = lax.broadcasted_iota(jnp.int32, (2 * t, 2 * t), 1)
    return jnp.where((j > s) | (s >= t), 1.0, 0.0).astype(BF16)


def _neg_abs(x):
    bits = lax.bitcast_convert_type(x, jnp.uint32) | jnp.uint32(0x80000000)
    return lax.bitcast_convert_type(bits, F32)


def _sb_front(qk, mask):
    zns = [_dot_nt(q, k) * (-(SB_HEAD_DIM ** -0.5) * LOG2_E) for q, k, _ in qk]
    keeps = [jnp.minimum(zn, 0.0) - jnp.log2(1.0 + jnp.exp2(_neg_abs(zn))) for zn in zns]
    betas = [keep - zn for keep, zn in zip(keeps, zns)]
    pieces = []
    for keep, (_, _, n_tiles) in zip(keeps, qk):
        per_tile = []
        for d in range(n_tiles):
            log_keep = keep[:, d * SB_TILE:(d + 1) * SB_TILE]
            if mask is not None and d == n_tiles - 1:
                log_keep = jnp.where(mask, log_keep, 0.0)
            hi = log_keep.astype(BF16)
            lo = (log_keep - hi.astype(F32)).astype(BF16)
            per_tile.append(jnp.concatenate([hi, lo], axis=1))
        pieces.append(per_tile)
    return betas, pieces


def _sb_back(betas, pieces, vs, carries, mask, u):
    sums = [[_dot(p, u) for p in per_tile] for per_tile in pieces]
    ws = []
    for beta, per_tile, carry in zip(betas, sums, carries):
        n_tiles = len(per_tile)
        w_tiles = [None] * n_tiles
        for d in reversed(range(n_tiles)):
            suffix, total = per_tile[d][:, :SB_TILE], per_tile[d][:, SB_TILE:]
            between = suffix if carry is None else carry + suffix
            w = jnp.exp2(beta[:, d * SB_TILE:(d + 1) * SB_TILE] + between)
            if mask is not None and d == n_tiles - 1:
                w = jnp.where(mask, w, 0.0)
            w_tiles[d] = w.astype(BF16)
            carry = total if carry is None else carry + total
        ws.append((jnp.concatenate(w_tiles, axis=1), carry))
    return [(_dot(w, v), carry) for (w, carry), v in zip(ws, vs)]


def _sb_chains(chains, mask, u):
    betas, pieces = _sb_front([(q, k, n) for q, k, _, n, _ in chains], mask)
    return _sb_back(betas, pieces, [c[2] for c in chains], [c[4] for c in chains], mask, u)


def _live(carry):
    return jnp.max(carry) >= SB_LOG2_WEIGHT_FLOOR


def _sb_prompt_kernel(q_ref, k_ref, v_ref, o_ref, acc_ref, carry_ref):
    t = SB_TILE
    u = _suffix_matrix()
    causal = (lax.broadcasted_iota(jnp.int32, (t, t), 1) < lax.broadcasted_iota(jnp.int32, (t, t), 0))

    def rows(tile, n=1):
        start = tile * t
        return pl.ds(start if isinstance(start, int) else pl.multiple_of(start, t), n * t)

    def keys(qt, n_tiles):
        return rows(qt - (n_tiles - 1), n_tiles)

    def settle(qt, sizes, results):
        def pending(j, carries):
            worst = None
            for s, n in enumerate(sizes):
                c = jnp.where(qt + s - n - j >= 0, carries[s], -jnp.inf)
                worst = c if worst is None else jnp.maximum(worst, c)
            return _live(worst)

        for s, (pv, carry) in enumerate(results):
            acc_ref[s] = pv
            carry_ref[s] = carry
            o_ref[rows(qt + s), :] = pv.astype(BF16)

        def body(state):
            j, _ = state
            kbs = [qt + s - n - j for s, n in enumerate(sizes)]
            tiles = [rows(jnp.maximum(kb, 0)) for kb in kbs]
            stepped = _sb_chains([(q_ref[rows(qt + s), :], k_ref[tiles[s], :], v_ref[tiles[s], :], 1,
                                   carry_ref[s]) for s in range(len(sizes))], None, u)
            carries = []
            for s, (pv, carry) in enumerate(stepped):
                acc = acc_ref[s] + jnp.where(kbs[s] >= 0, pv, 0.0)
                carry = jnp.where(kbs[s] >= 0, carry, carry_ref[s])
                acc_ref[s] = acc
                carry_ref[s] = carry
                o_ref[rows(qt + s), :] = acc.astype(BF16)
                carries.append(carry)
            return j + 1, pending(j + 1, carries)

        lax.while_loop(lambda state: state[1], body, (0, pending(0, [c for _, c in results])))

    def tile_group(qt, sizes):
        settle(qt, sizes, _sb_chains(
            [(q_ref[rows(qt + s), :], k_ref[keys(qt + s, n), :], v_ref[keys(qt + s, n), :], n, None)
             for s, n in enumerate(sizes)], causal, u))

    n_q = q_ref.shape[0] // t
    first = -(-(SB_WINDOW - 1) // SB_CHAINS) * SB_CHAINS
    for qt in range(0, first, SB_CHAINS):
        tile_group(qt, [min(qt + s + 1, SB_WINDOW) for s in range(SB_CHAINS)])

    def step(i, _):
        tile_group(first + SB_CHAINS * i, [SB_WINDOW] * SB_CHAINS)
        return 0

    lax.fori_loop(0, (n_q - first) // SB_CHAINS, step, 0)


def _sb_prompt(proj, *, batch, seq):
    assert seq % (SB_CHAINS * SB_TILE) == 0
    head = lambda c: pl.BlockSpec((seq, SB_HEAD_DIM), lambda b, h: (b, c * N_SB_HEADS + h))
    return pl.pallas_call(
        _sb_prompt_kernel,
        grid=(batch, N_SB_HEADS),
        in_specs=[head(COL_Q), head(COL_K), head(COL_V)],
        out_specs=pl.BlockSpec((seq, SB_HEAD_DIM), lambda b, h: (b, h)),
        out_shape=jax.ShapeDtypeStruct((batch * seq, N_SB_HEADS * SB_HEAD_DIM), BF16),
        scratch_shapes=[pltpu.VMEM((SB_CHAINS, SB_TILE, SB_HEAD_DIM), F32),
                        pltpu.VMEM((SB_CHAINS, SB_TILE, SB_TILE), F32)],
        compiler_params=_params("arbitrary", "arbitrary"),
        name="sb_prompt",
    )(proj, proj, proj)


def _sb_sample_kernel(q_ref, k_ref, v_ref, ck_hbm, cv_hbm, o_ref, kbuf, vbuf, sem, acc_ref, carry_ref,
                      *, cache_row0, past):
    t = SB_TILE
    n_new = q_ref.shape[0]
    n_blocks = past // t
    block_rows = t * N_SB_HEADS
    u = _suffix_matrix()
    base = cache_row0 + pl.program_id(0) * past * N_SB_HEADS

    def copies(kb, slot):
        src = pl.ds(base + kb * block_rows, block_rows)
        return (pltpu.make_async_copy(ck_hbm.at[src, :], kbuf.at[slot], sem.at[0, slot]),
                pltpu.make_async_copy(cv_hbm.at[src, :], vbuf.at[slot], sem.at[1, slot]))

    def start(kb, slot):
        for c in copies(kb, slot):
            c.start()

    def wait(kb, slot):
        for c in copies(kb, slot):
            c.wait()

    slots = kbuf.shape[0]
    ahead = slots - 1
    for kb in range(n_blocks - 1, max(n_blocks - 1 - ahead, -1), -1):
        start(kb, kb % slots)

    causal = (lax.broadcasted_iota(jnp.int32, (n_new, t), 1) < lax.broadcasted_iota(jnp.int32, (n_new, t), 0))
    pad = jnp.zeros((t - n_new, SB_HEAD_DIM), BF16)
    head_cols = [slice(h * SB_HEAD_DIM, (h + 1) * SB_HEAD_DIM) for h in range(N_SB_HEADS)]
    results = _sb_chains([(q_ref[:, cols], jnp.concatenate([k_ref[:, cols], pad], axis=0),
                           jnp.concatenate([v_ref[:, cols], pad], axis=0), 1, None)
                          for cols in head_cols], causal, u)
    live = False
    for h, (pv, carry) in enumerate(results):
        acc_ref[h] = pv
        carry_ref[h] = carry
        live = live | _live(carry)

    def body(state):
        kb, _ = state
        slot = kb % slots
        wait(kb, slot)

        @pl.when(kb >= ahead)
        def _():
            start(kb - ahead, (kb - ahead) % slots)

        head_rows = [pl.ds(h, t, stride=N_SB_HEADS) for h in range(N_SB_HEADS)]
        results = _sb_chains([(q_ref[:, head_cols[h]], kbuf[slot, head_rows[h], :].astype(BF16),
                               vbuf[slot, head_rows[h], :].astype(BF16), 1, carry_ref[h])
                              for h in range(N_SB_HEADS)], None, u)
        live = False
        for h, (pv, carry) in enumerate(results):
            acc_ref[h] += pv
            carry_ref[h] = carry
            live = live | _live(carry)
        return kb - 1, live

    kb_end, _ = lax.while_loop(lambda s: (s[0] >= 0) & s[1], body, (n_blocks - 1, live))

    for back in range(ahead):
        @pl.when(kb_end - back >= 0)
        def _():
            wait(kb_end - back, (kb_end - back) % slots)

    for h in range(N_SB_HEADS):
        o_ref[:, h * SB_HEAD_DIM:(h + 1) * SB_HEAD_DIM] = acc_ref[h].astype(BF16)


def _sb_sample(proj, cache_k, cache_v, *, layer, batch, seq, past):
    assert past % SB_TILE == 0 and seq <= SB_TILE
    blk = lambda c: pl.BlockSpec((seq, CHUNK_COLS), lambda b: (b, c))
    block_rows = SB_TILE * N_SB_HEADS
    return pl.pallas_call(
        functools.partial(_sb_sample_kernel, cache_row0=layer * batch * past * N_SB_HEADS, past=past),
        grid=(batch,),
        in_specs=[blk(COL_Q), blk(COL_K), blk(COL_V),
                  pl.BlockSpec(memory_space=pl.ANY), pl.BlockSpec(memory_space=pl.ANY)],
        out_specs=pl.BlockSpec((seq, N_SB_HEADS * SB_HEAD_DIM), lambda b: (b, 0)),
        out_shape=jax.ShapeDtypeStruct((batch * seq, N_SB_HEADS * SB_HEAD_DIM), BF16),
        scratch_shapes=[pltpu.VMEM((SB_CACHE_SLOTS, block_rows, SB_HEAD_DIM), F32),
                        pltpu.VMEM((SB_CACHE_SLOTS, block_rows, SB_HEAD_DIM), F32),
                        pltpu.SemaphoreType.DMA((2, SB_CACHE_SLOTS)),
                        pltpu.VMEM((N_SB_HEADS, seq, SB_HEAD_DIM), F32),
                        pltpu.VMEM((N_SB_HEADS, seq, SB_TILE), F32)],
        compiler_params=_params("arbitrary"),
        name="sb_sample",
    )(proj, proj, proj, cache_k, cache_v)


def _mixer_kernel(x_ref, osb_ref, u_ref, vn_ref, qm_ref, g0_ref, g1_ref, g2_ref, ws_ref, bs_ref,
                  mk_ref, mv_ref, wb_ref, wo_ref, gp_ref, y_ref, osg_ref, omem_ref, *, chunk, seg,
                  cache_rows):
    tm = x_ref.shape[0]

    def mem_head(ref, s, h):
        if not cache_rows:
            return ref[s, :, h * MEM_HEAD_DIM:(h + 1) * MEM_HEAD_DIM]
        halves = MEM_HEAD_DIM // SB_HEAD_DIM
        per_token = halves * N_MEM_HEADS
        return jnp.concatenate(
            [ref[pl.ds(s * N_MEM * per_token + c * N_MEM_HEADS + h, N_MEM, stride=per_token), :]
             for c in range(halves)], axis=1).astype(BF16)

    r = lax.broadcasted_iota(jnp.int32, (chunk, chunk), 0)
    c = lax.broadcasted_iota(jnp.int32, (chunk, chunk), 1)
    for g in range(N_SG_GROUPS):
        w_g = jnp.where(c <= r, ws_ref[g], 0.0).astype(BF16)
        b_g = bs_ref[:, g:g + 1]
        cols = slice(g * SG_GROUP_DIM, (g + 1) * SG_GROUP_DIM)
        for n in range(tm // chunk):
            rows = slice(n * chunk, (n + 1) * chunk)
            sg = _dot(w_g, vn_ref[rows, cols]) + b_g
            osg_ref[rows, cols] = (u_ref[rows, cols].astype(F32) * sg).astype(BF16)
    for s in range(tm // seg):
        rows = slice(s * seg, (s + 1) * seg)
        for h in range(N_MEM_HEADS):
            cols = slice(h * MEM_HEAD_DIM, (h + 1) * MEM_HEAD_DIM)
            sc = _dot_nt(qm_ref[rows, cols], mem_head(mk_ref, s, h)) * (MEM_HEAD_DIM ** -0.5)
            p = jnp.exp(sc - jnp.max(sc, axis=-1, keepdims=True))
            o = _dot(p.astype(BF16), mem_head(mv_ref, s, h)) / jnp.sum(p, axis=-1, keepdims=True)
            omem_ref[rows, cols] = o.astype(BF16)
    merged = (g0_ref[...].astype(F32) * _dot(osb_ref[...], wb_ref[0])
              + g1_ref[...].astype(F32) * _dot(osg_ref[...], wb_ref[1])
              + g2_ref[...].astype(F32) * _dot(omem_ref[...], wb_ref[2]))
    out = _dot(merged.astype(BF16), wo_ref[...])
    y_ref[...] = x_ref[...] + _rmsnorm(out, gp_ref[...])


def _mixer(x, o_sb, proj, w_s, b_s_t, mem_k, mem_v, w_branch, w_out, g_post, *, layer, tm, chunk, seg,
           rows_per_mem, mem_v_first=0, cache_rows=False):
    m = x.shape[0]
    row = lambda i: (i, 0)
    col = lambda cb: pl.BlockSpec((tm, CHUNK_COLS), lambda i: (i, cb))
    mem_blocks = tm // seg
    if cache_rows:
        assert m == tm
        layer_rows = mem_k.shape[0] // w_out.shape[0]
        mem = lambda first: pl.BlockSpec((layer_rows, SB_HEAD_DIM), lambda i: (layer, 0),
                                         pipeline_mode=pl.Buffered(1))
    else:
        mem = lambda first: pl.BlockSpec((mem_blocks, N_MEM, D_MODEL),
                                         lambda i: (first + i * tm // rows_per_mem // mem_blocks, 0, 0))
    const2 = lambda i: (0, 0)
    const3 = lambda i: (0, 0, 0)
    return pl.pallas_call(
        functools.partial(_mixer_kernel, chunk=chunk, seg=seg, cache_rows=cache_rows),
        grid=(m // tm,),
        in_specs=[pl.BlockSpec((tm, D_MODEL), row), pl.BlockSpec((tm, D_MODEL), row),
                  col(COL_U), col(COL_VSG), col(COL_QMEM), col(COL_GATE), col(COL_GATE + 1),
                  col(COL_GATE + 2),
                  pl.BlockSpec((N_SG_GROUPS, chunk, chunk), const3),
                  pl.BlockSpec((chunk, N_SG_GROUPS), const2),
                  mem(0), mem(mem_v_first),
                  _layer_weight(layer, N_BRANCH, D_MODEL, D_MODEL),
                  _layer_weight(layer, D_MODEL, D_MODEL),
                  pl.BlockSpec((1, D_MODEL), const2)],
        out_specs=pl.BlockSpec((tm, D_MODEL), row),
        out_shape=jax.ShapeDtypeStruct((m, D_MODEL), F32),
        scratch_shapes=[pltpu.VMEM((tm, D_MODEL), BF16), pltpu.VMEM((tm, D_MODEL), BF16)],
        compiler_params=_params("arbitrary"),
        name="mixer",
    )(x, o_sb, proj, proj, proj, proj, proj, proj, w_s, b_s_t, mem_k, mem_v, w_branch, w_out, g_post)


def _ffn_kernel(x_ref, gpre_ref, wi_ref, wo_ref, gpost_ref, y_ref, act_ref):
    tm = x_ref.shape[0]
    for r in range(0, tm, tm // ROW_PARTS):
        rows = slice(r, r + tm // ROW_PARTS)
        h = _rmsnorm(x_ref[rows, :], gpre_ref[...]).astype(BF16)
        for c in range(0, D_FF, FF_CHUNK):
            a = _dot(h, wi_ref[:, c:c + FF_CHUNK])
            b = _dot(h, wi_ref[:, D_FF + c:D_FF + c + FF_CHUNK])
            act_ref[rows, c:c + FF_CHUNK] = (a * _sigmoid(a) * b).astype(BF16)
        out = _dot(act_ref[rows, :], wo_ref[...])
        y_ref[rows, :] = x_ref[rows, :] + _rmsnorm(out, gpost_ref[...])


def _ffn(x, g_pre, w_ffn_in, w_ffn_out, g_post, *, layer, tm):
    m = x.shape[0]
    row = lambda i: (i, 0)
    resident = lambda shape: pl.BlockSpec(shape, lambda i: (0, 0), pipeline_mode=pl.Buffered(1))
    return pl.pallas_call(
        _ffn_kernel,
        grid=(m // tm,),
        in_specs=[pl.BlockSpec((tm, D_MODEL), row),
                  resident((1, D_MODEL)),
                  _layer_weight(layer, D_MODEL, 2 * D_FF),
                  _layer_weight(layer, D_FF, D_MODEL),
                  resident((1, D_MODEL))],
        out_specs=pl.BlockSpec((tm, D_MODEL), row),
        out_shape=jax.ShapeDtypeStruct((m, D_MODEL), F32),
        scratch_shapes=[pltpu.VMEM((tm, D_FF), BF16)],
        compiler_params=_params("arbitrary"),
        name="ffn",
    )(x, g_pre, w_ffn_in, w_ffn_out, g_post)


def _mem_kv_kernel(mem_ref, g_ref, w_ref, kv32_ref, kv16_ref):
    kv = _dot(_rmsnorm(mem_ref[...], g_ref[...]).astype(BF16), w_ref[...].astype(BF16))
    kv32_ref[0] = kv
    kv16_ref[0] = kv.astype(BF16)


def _mem_kv(mem, g_mem, w_mem_kv, *, layer):
    m = mem.shape[0]
    out = pl.BlockSpec((1, m, D_MODEL), lambda j: (j, 0, 0))
    return pl.pallas_call(
        _mem_kv_kernel,
        grid=(2,),
        in_specs=[pl.BlockSpec((m, D_MODEL), lambda j: (0, 0)),
                  pl.BlockSpec((1, D_MODEL), lambda j: (0, 0)),
                  pl.BlockSpec((None, D_MODEL, D_MODEL), lambda j: (layer, 0, j))],
        out_specs=[out, out],
        out_shape=[jax.ShapeDtypeStruct((2, m, D_MODEL), F32),
                   jax.ShapeDtypeStruct((2, m, D_MODEL), BF16)],
        compiler_params=_params("arbitrary"),
        name="mem_kv",
    )(mem, g_mem, w_mem_kv)


def kernel(x_prompt, x_sample, cache_sb_k, cache_sb_v, cache_mem_k, cache_mem_v, mem_prompt,
           g_pre_mix, w_in, b_gate, ln_sg_g, ln_sg_b, w_spatial, b_spatial, g_mem, w_mem_kv,
           w_branch, w_out, g_post_mix, g_pre_ffn, w_ffn_in, w_ffn_out, g_post_ffn):
    depth = w_in.shape[0]
    batch, seq, _ = x_prompt.shape
    dec_batch, dec_seq, _ = x_sample.shape
    past = cache_sb_k.shape[2]
    tm = 512
    tm_proj = 256
    m_s = dec_batch * dec_seq
    assert (batch * seq) % tm_proj == 0 and seq % tm == 0 and m_s == tm

    xp = x_prompt.reshape(batch * seq, D_MODEL)
    xs = x_sample.reshape(m_s, D_MODEL)
    mem = mem_prompt.reshape(batch * N_MEM, D_MODEL)
    cache_k = cache_sb_k.reshape(-1, SB_HEAD_DIM)
    cache_v = cache_sb_v.reshape(-1, SB_HEAD_DIM)

    def stored_rows(c):
        d, b, n, h, w = c.shape
        halves = w // SB_HEAD_DIM
        return (c.reshape(d, b, n, h, halves, SB_HEAD_DIM).transpose(0, 1, 2, 4, 3, 5)
                .reshape(-1, SB_HEAD_DIM))

    mem_cache_k, mem_cache_v = stored_rows(cache_mem_k), stored_rows(cache_mem_v)
    vec = lambda a: a.reshape(1, -1)
    mem_outs = [[] for _ in range(2)]
    prev_p, prev_s = (), ()
    w_in, w_branch, w_out, w_ffn_in, w_ffn_out = (
        w.astype(BF16) for w in (w_in, w_branch, w_out, w_ffn_in, w_ffn_out))
    for l in range(depth):
        proj_args = (vec(g_pre_mix[l]), w_in, vec(b_gate[l]), vec(ln_sg_g[l]), vec(ln_sg_b[l]))
        ffn_args = (vec(g_pre_ffn[l]), w_ffn_in, w_ffn_out, vec(g_post_ffn[l]))

        kv32, kv16 = _mem_kv(mem, vec(g_mem[l]), w_mem_kv, layer=l)
        proj, *prev_p = _in_proj(xp, *proj_args, prev_p, layer=l, depth=depth, tm=tm_proj, emit_vn=False)
        o_sb = _sb_prompt(proj, batch=batch, seq=seq)
        mem_kv16 = kv16.reshape(2 * batch, N_MEM, D_MODEL)
        xp = _mixer(xp, o_sb, proj, w_spatial[l], b_spatial[l].T, mem_kv16, mem_kv16,
                    w_branch, w_out, vec(g_post_mix[l]),
                    layer=l, tm=tm, chunk=SG_CHUNK, seg=tm, rows_per_mem=seq, mem_v_first=batch)
        xp = _ffn(xp, *ffn_args, layer=l, tm=2 * tm)
        mem_outs[0].append(kv32[0].reshape(batch, N_MEM, N_MEM_HEADS, MEM_HEAD_DIM))
        mem_outs[1].append(kv32[1].reshape(batch, N_MEM, N_MEM_HEADS, MEM_HEAD_DIM))

        proj, *prev_s = _in_proj(xs, *proj_args, prev_s, layer=l, depth=depth, tm=tm_proj, emit_vn=True)
        o_sb = _sb_sample(proj, cache_k, cache_v, layer=l, batch=dec_batch, seq=dec_seq, past=past)
        xs = _mixer(xs, o_sb, proj, w_spatial[l][:, :dec_seq, :dec_seq], b_spatial[l][:, :dec_seq].T,
                    mem_cache_k, mem_cache_v, w_branch, w_out, vec(g_post_mix[l]),
                    layer=l, tm=tm, chunk=dec_seq, seg=dec_seq, rows_per_mem=dec_seq, cache_rows=True)
        xs = _ffn(xs, *ffn_args, layer=l, tm=tm)

    heads = (N_SB_HEADS, SB_HEAD_DIM)
    return (xp.reshape(batch, seq, D_MODEL), xs.reshape(dec_batch, dec_seq, D_MODEL),
            prev_p[0].reshape(depth, batch, seq, *heads), prev_p[1].reshape(depth, batch, seq, *heads),
            jnp.stack(mem_outs[0]), jnp.stack(mem_outs[1]),
            prev_s[0].reshape(depth, dec_batch, dec_seq, *heads),
            prev_s[1].reshape(depth, dec_batch, dec_seq, *heads),
            prev_s[2].reshape(depth, dec_batch, dec_seq, D_MODEL))
```

```python
import functools

import jax
import jax.numpy as jnp
from jax import lax
from jax.experimental import pallas as pl
from jax.experimental.pallas import tpu as pltpu

F32 = jnp.float32
BF16 = jnp.bfloat16

D_MODEL = 1024
CHUNK_COLS = 1024
N_SB_HEADS = 8
SB_HEAD_DIM = 128
N_SG_GROUPS = 4
SG_GROUP_DIM = 256
SG_CHUNK = 128
N_MEM = 256
N_MEM_HEADS = 4
MEM_HEAD_DIM = 256
N_BRANCH = 3
D_FF = 2816
EPS = 1e-6

COL_Q, COL_K, COL_V, COL_U, COL_VSG, COL_QMEM, COL_GATE = 0, 1, 2, 3, 4, 5, 6
N_COL_BLOCKS = COL_GATE + N_BRANCH
IN_PROJ_ORDER = (COL_U, COL_Q, COL_VSG, COL_K, COL_GATE, COL_V, COL_GATE + 1, COL_GATE + 2, COL_QMEM)

SB_TILE = 128
SB_WINDOW = 3
SB_CACHE_SLOTS = 4
SB_CHAINS = 16
SB_LOG2_WEIGHT_FLOOR = -150.04
ROW_PARTS = 2
FF_CHUNK = 256
LOG2_E = 1.4426950408889634

VMEM_LIMIT = 56 * 1024 * 1024


def _dot(a, b):
    return jnp.dot(a, b, preferred_element_type=F32)


def _dot_nt(a, b):
    return lax.dot_general(a, b, (((1,), (1,)), ((), ())), preferred_element_type=F32)


def _rmsnorm(x, g):
    return x * lax.rsqrt(jnp.mean(x * x, axis=-1, keepdims=True) + EPS) * g


def _gelu(x):
    c, k = 0.7978845608028654, 0.044715
    half = 0.5 * x
    return half + half * jnp.tanh(x * (c + (c * k) * (x * x)))


def _sigmoid(x):
    return 0.5 + 0.5 * jnp.tanh(0.5 * x)


def _layer_weight(layer, *shape):
    zeros = (0,) * len(shape)
    return pl.BlockSpec((None, *shape), lambda *_: (layer, *zeros), pipeline_mode=pl.Buffered(1))


def _params(*semantics):
    return pltpu.CompilerParams(dimension_semantics=semantics, vmem_limit_bytes=VMEM_LIMIT)


def _store_heads(dst_ref, acc):
    tm = acc.shape[0]
    for h in range(N_SB_HEADS):
        head = acc[:, h * SB_HEAD_DIM:(h + 1) * SB_HEAD_DIM]
        dst_ref[0, pl.ds(h, tm, stride=N_SB_HEADS), :] = head
        for r in range(1, dst_ref.shape[0]):
            dst_ref[r, pl.ds(h * tm, tm), :] = head


def _in_proj_kernel(*refs, n_prev, layer):
    (x_ref, g_ref, w_hbm, bg_ref, lng_ref, lnb_ref), refs = refs[:6], refs[6 + n_prev:]
    proj_ref, k_ref, v_ref, vn_ref, w_ref, sem = refs

    def fetch(j):
        return pltpu.make_async_copy(w_hbm.at[layer, j], w_ref.at[j], sem.at[j])

    def run(first_step):
        if first_step:
            for j in IN_PROJ_ORDER:
                fetch(j).start()
        h = _rmsnorm(x_ref[...], g_ref[...]).astype(BF16)
        for j in IN_PROJ_ORDER:
            cols = slice(j * CHUNK_COLS, (j + 1) * CHUNK_COLS)
            if first_step:
                fetch(j).wait()
            acc = _dot(h, w_ref[j])
            if j == COL_K:
                _store_heads(k_ref, acc)
            elif j == COL_V:
                _store_heads(v_ref, acc)
            elif j == COL_U:
                acc = _gelu(acc)
            elif j == COL_VSG:
                a = _gelu(acc)
                mu = jnp.mean(a, axis=-1, keepdims=True)
                d = a - mu
                var = jnp.mean(d * d, axis=-1, keepdims=True)
                acc = d * lax.rsqrt(var + EPS) * lng_ref[...] + lnb_ref[...]
                if vn_ref is not None:
                    for r in range(vn_ref.shape[0]):
                        vn_ref[r] = acc
            elif j >= COL_GATE:
                g = slice((j - COL_GATE) * CHUNK_COLS, (j - COL_GATE + 1) * CHUNK_COLS)
                acc = _sigmoid(acc + bg_ref[:, g])
            proj_ref[:, cols] = acc.astype(BF16)

    pl.when(pl.program_id(0) == 0)(lambda: run(True))
    pl.when(pl.program_id(0) > 0)(lambda: run(False))


def _in_proj(x, g_pre, w_in, b_gate, ln_g, ln_b, prev, *, layer, depth, tm, emit_vn):
    m = x.shape[0]
    n_i = m // tm
    d_in = N_COL_BLOCKS * CHUNK_COLS
    row = lambda i: (i, 0)
    resident = lambda shape: pl.BlockSpec(shape, lambda i: (0, 0), pipeline_mode=pl.Buffered(1))
    layers = depth if layer == 0 else 1
    layer_rows = lambda shape: pl.BlockSpec((layers, *shape), lambda i: (layer, i, 0))
    out_shape = [jax.ShapeDtypeStruct((m, d_in), BF16),
                 jax.ShapeDtypeStruct((depth, m * N_SB_HEADS, SB_HEAD_DIM), F32),
                 jax.ShapeDtypeStruct((depth, m * N_SB_HEADS, SB_HEAD_DIM), F32)]
    out_specs = [pl.BlockSpec((tm, d_in), row),
                 layer_rows((tm * N_SB_HEADS, SB_HEAD_DIM)),
                 layer_rows((tm * N_SB_HEADS, SB_HEAD_DIM))]
    if emit_vn:
        out_shape.append(jax.ShapeDtypeStruct((depth, m, CHUNK_COLS), F32))
        out_specs.append(layer_rows((tm, CHUNK_COLS)))
    n_prev = len(prev)
    assert n_prev == (0 if layer == 0 else len(out_shape) - 1)

    def body(*refs):
        if not emit_vn:
            refs = refs[:-2] + (None,) + refs[-2:]
        _in_proj_kernel(*refs, n_prev=n_prev, layer=layer)

    return pl.pallas_call(
        body,
        grid=(n_i,),
        in_specs=[pl.BlockSpec((tm, D_MODEL), row),
                  resident((1, D_MODEL)),
                  pl.BlockSpec(memory_space=pl.ANY),
                  resident((1, N_BRANCH * CHUNK_COLS)),
                  resident((1, CHUNK_COLS)),
                  resident((1, CHUNK_COLS))]
                 + [pl.BlockSpec(memory_space=pl.ANY)] * n_prev,
        out_specs=out_specs,
        out_shape=out_shape,
        input_output_aliases={6 + p: 1 + p for p in range(n_prev)},
        scratch_shapes=[pltpu.VMEM((N_COL_BLOCKS, D_MODEL, CHUNK_COLS), BF16),
                        pltpu.SemaphoreType.DMA((N_COL_BLOCKS,))],
        compiler_params=_params("arbitrary"),
        name="in_proj",
    )(x, g_pre, w_in, b_gate, ln_g, ln_b, *prev)


def _suffix_matrix():
    t = SB_TILE
    j = lax.broadcasted_iota(jnp.int32, (2 * t, 2 * t), 0) % t
    s = lax.broadcasted_iota(jnp.int32, (2 * t, 2 * t), 1)
    return jnp.where((j > s) | (s >= t), 1.0, 0.0).astype(BF16)


def _sb_front(qk, mask):
    zns = [_dot_nt(q, k) * (-(SB_HEAD_DIM ** -0.5) * LOG2_E) for q, k, _ in qk]
    keeps = [jnp.minimum(zn, 0.0) - jnp.log2(1.0 + jnp.exp2(-jnp.abs(zn))) for zn in zns]
    betas = [keep - zn for keep, zn in zip(keeps, zns)]
    pieces = []
    for keep, (_, _, n_tiles) in zip(keeps, qk):
        per_tile = []
        for d in range(n_tiles):
            log_keep = keep[:, d * SB_TILE:(d + 1) * SB_TILE]
            if mask is not None and d == n_tiles - 1:
                log_keep = jnp.where(mask, log_keep, 0.0)
            hi = log_keep.astype(BF16)
            lo = (log_keep - hi.astype(F32)).astype(BF16)
            per_tile.append(jnp.concatenate([hi, lo], axis=1))
        pieces.append(per_tile)
    return betas, pieces


def _sb_back(betas, pieces, vs, carries, mask, u):
    sums = [[_dot(p, u) for p in per_tile] for per_tile in pieces]
    ws = []
    for beta, per_tile, carry in zip(betas, sums, carries):
        n_tiles = len(per_tile)
        w_tiles = [None] * n_tiles
        for d in reversed(range(n_tiles)):
            suffix, total = per_tile[d][:, :SB_TILE], per_tile[d][:, SB_TILE:]
            between = suffix if carry is None else carry + suffix
            w = jnp.exp2(beta[:, d * SB_TILE:(d + 1) * SB_TILE] + between)
            if mask is not None and d == n_tiles - 1:
                w = jnp.where(mask, w, 0.0)
            w_tiles[d] = w.astype(BF16)
            carry = total if carry is None else carry + total
        ws.append((jnp.concatenate(w_tiles, axis=1), carry))
    return [(_dot(w, v), carry) for (w, carry), v in zip(ws, vs)]


def _sb_chains(chains, mask, u):
    betas, pieces = _sb_front([(q, k, n) for q, k, _, n, _ in chains], mask)
    return _sb_back(betas, pieces, [c[2] for c in chains], [c[4] for c in chains], mask, u)


def _live(carry):
    return jnp.max(carry) >= SB_LOG2_WEIGHT_FLOOR


def _sb_prompt_kernel(q_ref, k_ref, v_ref, o_ref, acc_ref, carry_ref):
    t = SB_TILE
    u = _suffix_matrix()
    causal = (lax.broadcasted_iota(jnp.int32, (t, t), 1) < lax.broadcasted_iota(jnp.int32, (t, t), 0))

    def rows(tile, n=1):
        start = tile * t
        return pl.ds(start if isinstance(start, int) else pl.multiple_of(start, t), n * t)

    def keys(qt, n_tiles):
        return rows(qt - (n_tiles - 1), n_tiles)

    def settle(qt, sizes, results):
        def pending(j, carries):
            worst = None
            for s, n in enumerate(sizes):
                c = jnp.where(qt + s - n - j >= 0, carries[s], -jnp.inf)
                worst = c if worst is None else jnp.maximum(worst, c)
            return _live(worst)

        for s, (pv, carry) in enumerate(results):
            acc_ref[s] = pv
            carry_ref[s] = carry
            o_ref[rows(qt + s), :] = pv.astype(BF16)

        def body(state):
            j, _ = state
            kbs = [qt + s - n - j for s, n in enumerate(sizes)]
            tiles = [rows(jnp.maximum(kb, 0)) for kb in kbs]
            stepped = _sb_chains([(q_ref[rows(qt + s), :], k_ref[tiles[s], :], v_ref[tiles[s], :], 1,
                                   carry_ref[s]) for s in range(len(sizes))], None, u)
            carries = []
            for s, (pv, carry) in enumerate(stepped):
                acc = acc_ref[s] + jnp.where(kbs[s] >= 0, pv, 0.0)
                carry = jnp.where(kbs[s] >= 0, carry, carry_ref[s])
                acc_ref[s] = acc
                carry_ref[s] = carry
                o_ref[rows(qt + s), :] = acc.astype(BF16)
                carries.append(carry)
            return j + 1, pending(j + 1, carries)

        lax.while_loop(lambda state: state[1], body, (0, pending(0, [c for _, c in results])))

    def tile_group(qt, sizes):
        settle(qt, sizes, _sb_chains(
            [(q_ref[rows(qt + s), :], k_ref[keys(qt + s, n), :], v_ref[keys(qt + s, n), :], n, None)
             for s, n in enumerate(sizes)], causal, u))

    n_q = q_ref.shape[0] // t
    first = -(-(SB_WINDOW - 1) // SB_CHAINS) * SB_CHAINS
    for qt in range(0, first, SB_CHAINS):
        tile_group(qt, [min(qt + s + 1, SB_WINDOW) for s in range(SB_CHAINS)])

    def step(i, _):
        tile_group(first + SB_CHAINS * i, [SB_WINDOW] * SB_CHAINS)
        return 0

    lax.fori_loop(0, (n_q - first) // SB_CHAINS, step, 0)


def _sb_prompt(proj, *, batch, seq):
    assert seq % (SB_CHAINS * SB_TILE) == 0
    head = lambda c: pl.BlockSpec((seq, SB_HEAD_DIM), lambda b, h: (b, c * N_SB_HEADS + h))
    return pl.pallas_call(
        _sb_prompt_kernel,
        grid=(batch, N_SB_HEADS),
        in_specs=[head(COL_Q), head(COL_K), head(COL_V)],
        out_specs=pl.BlockSpec((seq, SB_HEAD_DIM), lambda b, h: (b, h)),
        out_shape=jax.ShapeDtypeStruct((batch * seq, N_SB_HEADS * SB_HEAD_DIM), BF16),
        scratch_shapes=[pltpu.VMEM((SB_CHAINS, SB_TILE, SB_HEAD_DIM), F32),
                        pltpu.VMEM((SB_CHAINS, SB_TILE, SB_TILE), F32)],
        compiler_params=_params("arbitrary", "arbitrary"),
        name="sb_prompt",
    )(proj, proj, proj)


def _sb_sample_kernel(q_ref, k_ref, v_ref, ck_hbm, cv_hbm, o_ref, kbuf, vbuf, sem, acc_ref, carry_ref,
                      *, cache_row0, past):
    t = SB_TILE
    n_new = q_ref.shape[0]
    n_blocks = past // t
    block_rows = t * N_SB_HEADS
    u = _suffix_matrix()
    base = cache_row0 + pl.program_id(0) * past * N_SB_HEADS

    def copies(kb, slot):
        src = pl.ds(base + kb * block_rows, block_rows)
        return (pltpu.make_async_copy(ck_hbm.at[src, :], kbuf.at[slot], sem.at[0, slot]),
                pltpu.make_async_copy(cv_hbm.at[src, :], vbuf.at[slot], sem.at[1, slot]))

    def start(kb, slot):
        for c in copies(kb, slot):
            c.start()

    def wait(kb, slot):
        for c in copies(kb, slot):
            c.wait()

    slots = kbuf.shape[0]
    ahead = slots - 1
    for kb in range(n_blocks - 1, max(n_blocks - 1 - ahead, -1), -1):
        start(kb, kb % slots)

    causal = (lax.broadcasted_iota(jnp.int32, (n_new, t), 1) < lax.broadcasted_iota(jnp.int32, (n_new, t), 0))
    pad = jnp.zeros((t - n_new, SB_HEAD_DIM), BF16)
    head_cols = [slice(h * SB_HEAD_DIM, (h + 1) * SB_HEAD_DIM) for h in range(N_SB_HEADS)]
    results = _sb_chains([(q_ref[:, cols], jnp.concatenate([k_ref[:, cols], pad], axis=0),
                           jnp.concatenate([v_ref[:, cols], pad], axis=0), 1, None)
                          for cols in head_cols], causal, u)
    live = False
    for h, (pv, carry) in enumerate(results):
        acc_ref[h] = pv
        carry_ref[h] = carry
        live = live | _live(carry)

    def body(state):
        kb, _ = state
        slot = kb % slots
        wait(kb, slot)

        @pl.when(kb >= ahead)
        def _():
            start(kb - ahead, (kb - ahead) % slots)

        head_rows = [pl.ds(h, t, stride=N_SB_HEADS) for h in range(N_SB_HEADS)]
        results = _sb_chains([(q_ref[:, head_cols[h]], kbuf[slot, head_rows[h], :].astype(BF16),
                               vbuf[slot, head_rows[h], :].astype(BF16), 1, carry_ref[h])
                              for h in range(N_SB_HEADS)], None, u)
        live = False
        for h, (pv, carry) in enumerate(results):
            acc_ref[h] += pv
            carry_ref[h] = carry
            live = live | _live(carry)
        return kb - 1, live

    kb_end, _ = lax.while_loop(lambda s: (s[0] >= 0) & s[1], body, (n_blocks - 1, live))

    for back in range(ahead):
        @pl.when(kb_end - back >= 0)
        def _():
            wait(kb_end - back, (kb_end - back) % slots)

    for h in range(N_SB_HEADS):
        o_ref[:, h * SB_HEAD_DIM:(h + 1) * SB_HEAD_DIM] = acc_ref[h].astype(BF16)


def _sb_sample(proj, cache_k, cache_v, *, layer, batch, seq, past):
    assert past % SB_TILE == 0 and seq <= SB_TILE
    blk = lambda c: pl.BlockSpec((seq, CHUNK_COLS), lambda b: (b, c))
    block_rows = SB_TILE * N_SB_HEADS
    return pl.pallas_call(
        functools.partial(_sb_sample_kernel, cache_row0=layer * batch * past * N_SB_HEADS, past=past),
        grid=(batch,),
        in_specs=[blk(COL_Q), blk(COL_K), blk(COL_V),
                  pl.BlockSpec(memory_space=pl.ANY), pl.BlockSpec(memory_space=pl.ANY)],
        out_specs=pl.BlockSpec((seq, N_SB_HEADS * SB_HEAD_DIM), lambda b: (b, 0)),
        out_shape=jax.ShapeDtypeStruct((batch * seq, N_SB_HEADS * SB_HEAD_DIM), BF16),
        scratch_shapes=[pltpu.VMEM((SB_CACHE_SLOTS, block_rows, SB_HEAD_DIM), F32),
                        pltpu.VMEM((SB_CACHE_SLOTS, block_rows, SB_HEAD_DIM), F32),
                        pltpu.SemaphoreType.DMA((2, SB_CACHE_SLOTS)),
                        pltpu.VMEM((N_SB_HEADS, seq, SB_HEAD_DIM), F32),
                        pltpu.VMEM((N_SB_HEADS, seq, SB_TILE), F32)],
        compiler_params=_params("arbitrary"),
        name="sb_sample",
    )(proj, proj, proj, cache_k, cache_v)


def _mixer_kernel(x_ref, osb_ref, u_ref, vn_ref, qm_ref, g0_ref, g1_ref, g2_ref, ws_ref, bs_ref,
                  mk_ref, mv_ref, wb_ref, wo_ref, gp_ref, y_ref, osg_ref, omem_ref, *, chunk, seg,
                  cache_rows):
    tm = x_ref.shape[0]

    def mem_head(ref, s, h):
        if not cache_rows:
            return ref[s, :, h * MEM_HEAD_DIM:(h + 1) * MEM_HEAD_DIM]
        halves = MEM_HEAD_DIM // SB_HEAD_DIM
        per_token = halves * N_MEM_HEADS
        return jnp.concatenate(
            [ref[pl.ds(s * N_MEM * per_token + c * N_MEM_HEADS + h, N_MEM, stride=per_token), :]
             for c in range(halves)], axis=1).astype(BF16)

    r = lax.broadcasted_iota(jnp.int32, (chunk, chunk), 0)
    c = lax.broadcasted_iota(jnp.int32, (chunk, chunk), 1)
    for g in range(N_SG_GROUPS):
        w_g = jnp.where(c <= r, ws_ref[g], 0.0).astype(BF16)
        b_g = bs_ref[:, g:g + 1]
        cols = slice(g * SG_GROUP_DIM, (g + 1) * SG_GROUP_DIM)
        for n in range(tm // chunk):
            rows = slice(n * chunk, (n + 1) * chunk)
            sg = _dot(w_g, vn_ref[rows, cols]) + b_g
            osg_ref[rows, cols] = (u_ref[rows, cols].astype(F32) * sg).astype(BF16)
    for s in range(tm // seg):
        rows = slice(s * seg, (s + 1) * seg)
        for h in range(N_MEM_HEADS):
            cols = slice(h * MEM_HEAD_DIM, (h + 1) * MEM_HEAD_DIM)
            sc = _dot_nt(qm_ref[rows, cols], mem_head(mk_ref, s, h)) * (MEM_HEAD_DIM ** -0.5)
            p = jnp.exp(sc - jnp.max(sc, axis=-1, keepdims=True))
            o = _dot(p.astype(BF16), mem_head(mv_ref, s, h)) / jnp.sum(p, axis=-1, keepdims=True)
            omem_ref[rows, cols] = o.astype(BF16)
    merged = (g0_ref[...].astype(F32) * _dot(osb_ref[...], wb_ref[0])
              + g1_ref[...].astype(F32) * _dot(osg_ref[...], wb_ref[1])
              + g2_ref[...].astype(F32) * _dot(omem_ref[...], wb_ref[2]))
    out = _dot(merged.astype(BF16), wo_ref[...])
    y_ref[...] = x_ref[...] + _rmsnorm(out, gp_ref[...])


def _mixer(x, o_sb, proj, w_s, b_s_t, mem_k, mem_v, w_branch, w_out, g_post, *, layer, tm, chunk, seg,
           rows_per_mem, mem_v_first=0, cache_rows=False):
    m = x.shape[0]
    row = lambda i: (i, 0)
    col = lambda cb: pl.BlockSpec((tm, CHUNK_COLS), lambda i: (i, cb))
    mem_blocks = tm // seg
    if cache_rows:
        assert m == tm
        layer_rows = mem_k.shape[0] // w_out.shape[0]
        mem = lambda first: pl.BlockSpec((layer_rows, SB_HEAD_DIM), lambda i: (layer, 0),
                                         pipeline_mode=pl.Buffered(1))
    else:
        mem = lambda first: pl.BlockSpec((mem_blocks, N_MEM, D_MODEL),
                                         lambda i: (first + i * tm // rows_per_mem // mem_blocks, 0, 0))
    const2 = lambda i: (0, 0)
    const3 = lambda i: (0, 0, 0)
    return pl.pallas_call(
        functools.partial(_mixer_kernel, chunk=chunk, seg=seg, cache_rows=cache_rows),
        grid=(m // tm,),
        in_specs=[pl.BlockSpec((tm, D_MODEL), row), pl.BlockSpec((tm, D_MODEL), row),
                  col(COL_U), col(COL_VSG), col(COL_QMEM), col(COL_GATE), col(COL_GATE + 1),
                  col(COL_GATE + 2),
                  pl.BlockSpec((N_SG_GROUPS, chunk, chunk), const3),
                  pl.BlockSpec((chunk, N_SG_GROUPS), const2),
                  mem(0), mem(mem_v_first),
                  _layer_weight(layer, N_BRANCH, D_MODEL, D_MODEL),
                  _layer_weight(layer, D_MODEL, D_MODEL),
                  pl.BlockSpec((1, D_MODEL), const2)],
        out_specs=pl.BlockSpec((tm, D_MODEL), row),
        out_shape=jax.ShapeDtypeStruct((m, D_MODEL), F32),
        scratch_shapes=[pltpu.VMEM((tm, D_MODEL), BF16), pltpu.VMEM((tm, D_MODEL), BF16)],
        compiler_params=_params("arbitrary"),
        name="mixer",
    )(x, o_sb, proj, proj, proj, proj, proj, proj, w_s, b_s_t, mem_k, mem_v, w_branch, w_out, g_post)


def _ffn_kernel(x_ref, gpre_ref, wi_ref, wo_ref, gpost_ref, y_ref, act_ref):
    tm = x_ref.shape[0]
    for r in range(0, tm, tm // ROW_PARTS):
        rows = slice(r, r + tm // ROW_PARTS)
        h = _rmsnorm(x_ref[rows, :], gpre_ref[...]).astype(BF16)
        for c in range(0, D_FF, FF_CHUNK):
            a = _dot(h, wi_ref[:, c:c + FF_CHUNK])
            b = _dot(h, wi_ref[:, D_FF + c:D_FF + c + FF_CHUNK])
            act_ref[rows, c:c + FF_CHUNK] = (a * _sigmoid(a) * b).astype(BF16)
        out = _dot(act_ref[rows, :], wo_ref[...])
        y_ref[rows, :] = x_ref[rows, :] + _rmsnorm(out, gpost_ref[...])


def _ffn(x, g_pre, w_ffn_in, w_ffn_out, g_post, *, layer, tm):
    m = x.shape[0]
    row = lambda i: (i, 0)
    resident = lambda shape: pl.BlockSpec(shape, lambda i: (0, 0), pipeline_mode=pl.Buffered(1))
    return pl.pallas_call(
        _ffn_kernel,
        grid=(m // tm,),
        in_specs=[pl.BlockSpec((tm, D_MODEL), row),
                  resident((1, D_MODEL)),
                  _layer_weight(layer, D_MODEL, 2 * D_FF),
                  _layer_weight(layer, D_FF, D_MODEL),
                  resident((1, D_MODEL))],
        out_specs=pl.BlockSpec((tm, D_MODEL), row),
        out_shape=jax.ShapeDtypeStruct((m, D_MODEL), F32),
        scratch_shapes=[pltpu.VMEM((tm, D_FF), BF16)],
        compiler_params=_params("arbitrary"),
        name="ffn",
    )(x, g_pre, w_ffn_in, w_ffn_out, g_post)


def _mem_kv_kernel(mem_ref, g_ref, w_ref, kv32_ref, kv16_ref):
    kv = _dot(_rmsnorm(mem_ref[...], g_ref[...]).astype(BF16), w_ref[...].astype(BF16))
    kv32_ref[0] = kv
    kv16_ref[0] = kv.astype(BF16)


def _mem_kv(mem, g_mem, w_mem_kv, *, layer):
    m = mem.shape[0]
    out = pl.BlockSpec((1, m, D_MODEL), lambda j: (j, 0, 0))
    return pl.pallas_call(
        _mem_kv_kernel,
        grid=(2,),
        in_specs=[pl.BlockSpec((m, D_MODEL), lambda j: (0, 0)),
                  pl.BlockSpec((1, D_MODEL), lambda j: (0, 0)),
                  pl.BlockSpec((None, D_MODEL, D_MODEL), lambda j: (layer, 0, j))],
        out_specs=[out, out],
        out_shape=[jax.ShapeDtypeStruct((2, m, D_MODEL), F32),
                   jax.ShapeDtypeStruct((2, m, D_MODEL), BF16)],
        compiler_params=_params("arbitrary"),
        name="mem_kv",
    )(mem, g_mem, w_mem_kv)


def kernel(x_prompt, x_sample, cache_sb_k, cache_sb_v, cache_mem_k, cache_mem_v, mem_prompt,
           g_pre_mix, w_in, b_gate, ln_sg_g, ln_sg_b, w_spatial, b_spatial, g_mem, w_mem_kv,
           w_branch, w_out, g_post_mix, g_pre_ffn, w_ffn_in, w_ffn_out, g_post_ffn):
    depth = w_in.shape[0]
    batch, seq, _ = x_prompt.shape
    dec_batch, dec_seq, _ = x_sample.shape
    past = cache_sb_k.shape[2]
    tm = 512
    tm_proj = 256
    m_s = dec_batch * dec_seq
    assert (batch * seq) % tm_proj == 0 and seq % tm == 0 and m_s == tm

    xp = x_prompt.reshape(batch * seq, D_MODEL)
    xs = x_sample.reshape(m_s, D_MODEL)
    mem = mem_prompt.reshape(batch * N_MEM, D_MODEL)
    cache_k = cache_sb_k.reshape(-1, SB_HEAD_DIM)
    cache_v = cache_sb_v.reshape(-1, SB_HEAD_DIM)

    def stored_rows(c):
        d, b, n, h, w = c.shape
        halves = w // SB_HEAD_DIM
        return (c.reshape(d, b, n, h, halves, SB_HEAD_DIM).transpose(0, 1, 2, 4, 3, 5)
                .reshape(-1, SB_HEAD_DIM))

    mem_cache_k, mem_cache_v = stored_rows(cache_mem_k), stored_rows(cache_mem_v)
    vec = lambda a: a.reshape(1, -1)
    mem_outs = [[] for _ in range(2)]
    prev_p, prev_s = (), ()
    w_in, w_branch, w_out, w_ffn_in, w_ffn_out = (
        w.astype(BF16) for w in (w_in, w_branch, w_out, w_ffn_in, w_ffn_out))
    w_in = w_in.reshape(depth, D_MODEL, N_COL_BLOCKS, CHUNK_COLS).transpose(0, 2, 1, 3)
    for l in range(depth):
        proj_args = (vec(g_pre_mix[l]), w_in, vec(b_gate[l]), vec(ln_sg_g[l]), vec(ln_sg_b[l]))
        ffn_args = (vec(g_pre_ffn[l]), w_ffn_in, w_ffn_out, vec(g_post_ffn[l]))

        kv32, kv16 = _mem_kv(mem, vec(g_mem[l]), w_mem_kv, layer=l)
        proj, *prev_p = _in_proj(xp, *proj_args, prev_p, layer=l, depth=depth, tm=tm_proj, emit_vn=False)
        o_sb = _sb_prompt(proj, batch=batch, seq=seq)
        mem_kv16 = kv16.reshape(2 * batch, N_MEM, D_MODEL)
        xp = _mixer(xp, o_sb, proj, w_spatial[l], b_spatial[l].T, mem_kv16, mem_kv16,
                    w_branch, w_out, vec(g_post_mix[l]),
                    layer=l, tm=tm, chunk=SG_CHUNK, seg=tm, rows_per_mem=seq, mem_v_first=batch)
        xp = _ffn(xp, *ffn_args, layer=l, tm=2 * tm)
        mem_outs[0].append(kv32[0].reshape(batch, N_MEM, N_MEM_HEADS, MEM_HEAD_DIM))
        mem_outs[1].append(kv32[1].reshape(batch, N_MEM, N_MEM_HEADS, MEM_HEAD_DIM))

        proj, *prev_s = _in_proj(xs, *proj_args, prev_s, layer=l, depth=depth, tm=tm_proj, emit_vn=True)
        o_sb = _sb_sample(proj, cache_k, cache_v, layer=l, batch=dec_batch, seq=dec_seq, past=past)
        xs = _mixer(xs, o_sb, proj, w_spatial[l][:, :dec_seq, :dec_seq], b_spatial[l][:, :dec_seq].T,
                    mem_cache_k, mem_cache_v, w_branch, w_out, vec(g_post_mix[l]),
                    layer=l, tm=tm, chunk=dec_seq, seg=dec_seq, rows_per_mem=dec_seq, cache_rows=True)
        xs = _ffn(xs, *ffn_args, layer=l, tm=tm)

    heads = (N_SB_HEADS, SB_HEAD_DIM)
    return (xp.reshape(batch, seq, D_MODEL), xs.reshape(dec_batch, dec_seq, D_MODEL),
            prev_p[0].reshape(depth, batch, seq, *heads), prev_p[1].reshape(depth, batch, seq, *heads),
            jnp.stack(mem_outs[0]), jnp.stack(mem_outs[1]),
            prev_s[0].reshape(depth, dec_batch, dec_seq, *heads),
            prev_s[1].reshape(depth, dec_batch, dec_seq, *heads),
            prev_s[2].reshape(depth, dec_batch, dec_seq, D_MODEL))
```
